```python
import math, functools
import jax, jax.numpy as jnp
from jax import lax
import numpy as np

D_MODEL = 1024
BATCH = 32
SEQ = 2048
DEPTH = 2
DEC_BATCH = 8
DEC_SEQ = 32
PAST_LEN = 4096

CHUNK = 64
N_EVEN = (DEPTH + 1) // 2
N_ODD = DEPTH // 2
EPS = 1e-6
CONV_W = 4
A_HEADS = 4
A_DK = 128
A_DV = 128
A_WIDTH = A_HEADS * A_DV
B_WIDTH = 512
B_BLOCKS = 8
B_BLOCK = B_WIDTH // B_BLOCKS
RG_C = 8.0
AB_IN = 4 * A_WIDTH + 2 * A_HEADS + 2 * B_WIDTH
AB_SPLITS = [3 * A_WIDTH, 4 * A_WIDTH, 4 * A_WIDTH + A_HEADS, 4 * A_WIDTH + 2 * A_HEADS,
             4 * A_WIDTH + 2 * A_HEADS + B_WIDTH]
C_HEADS = 8
C_DH = 64
C_VD = 2 * C_DH
C_WIDTH = C_HEADS * C_VD
Q_BLOCK = 128
PEER_HEADS = 8
N_KEYS = 128
N_EXPERTS = N_KEYS * N_KEYS
PEER_TOPK = 16
PEER_DQ = 256
PEER_BLOCK = 256

kernel_name = 'hybrid_gdn_rglru_diffattn_peer_stream_step'


def rmsnorm(x, g):
    x32 = x.astype(jnp.float32)
    y = x32 * lax.rsqrt(jnp.mean(x32 * x32, axis=-1, keepdims=True) + EPS)
    return (y * g.astype(jnp.float32)).astype(x.dtype)


def l2norm(x):
    x32 = x.astype(jnp.float32)
    return x32 * lax.rsqrt(jnp.sum(x32 * x32, axis=-1, keepdims=True) + EPS)


def causal_conv(x, prev, w):
    t = x.shape[1]
    xp = jnp.concatenate([prev.astype(x.dtype), x], axis=1)
    out = sum(xp[:, j:j + t] * w[j] for j in range(CONV_W))
    return out, xp[:, -(CONV_W - 1):]


def gated_delta_rule(q, k, v, g, beta, s0, chunk):
    f32 = jnp.float32
    bsz, t, nh, _ = q.shape
    dv = v.shape[-1]
    n = t // chunk

    def to_chunks(a):
        a = a.astype(f32).reshape((bsz, n, chunk, nh) + a.shape[3:])
        return jnp.moveaxis(a, (1, 3), (0, 2))

    q, k, v, g, beta = (to_chunks(a) for a in (q, k, v, g, beta))
    gc = jnp.cumsum(g, axis=-1)
    pos = jnp.arange(chunk)
    lower = pos[:, None] >= pos[None, :]
    decay = jnp.exp(jnp.where(lower, gc[..., :, None] - gc[..., None, :], -jnp.inf))
    kb = k * beta[..., None]
    a_strict = jnp.where(pos[:, None] > pos[None, :],
                         jnp.einsum('nbhid,nbhjd->nbhij', kb, k) * decay, 0.0)
    lhs = a_strict + jnp.eye(chunk, dtype=f32)
    solve = functools.partial(lax.linalg.triangular_solve, left_side=True, lower=True)
    u = solve(lhs, v * beta[..., None])
    w = solve(lhs, kb * jnp.exp(gc)[..., None])
    qk = jnp.einsum('nbhid,nbhjd->nbhij', q, k) * decay

    def step(s, xs):
        q_c, k_c, u_c, w_c, gc_c, qk_c = xs
        v_new = u_c - jnp.einsum('bhck,bhkv->bhcv', w_c, s)
        o_c = (jnp.einsum('bhck,bhkv->bhcv', q_c * jnp.exp(gc_c)[..., None], s)
               + jnp.einsum('bhij,bhjv->bhiv', qk_c, v_new))
        g_last = gc_c[..., -1:]
        s = (s * jnp.exp(g_last)[..., None]
             + jnp.einsum('bhck,bhcv->bhkv', k_c * jnp.exp(g_last - gc_c)[..., None], v_new))
        return s, o_c

    s_fin, o = lax.scan(step, s0.astype(f32), (q, k, u, w, gc, qk))
    o = jnp.moveaxis(o, (0, 2), (1, 3)).reshape(bsz, t, nh, dv)
    return o, s_fin


def lru_combine(c1, c2):
    a1, b1 = c1
    a2, b2 = c2
    return a1 * a2, a2 * b1 + b2


def mixer_ab(h, conv_a_prev, s0, conv_b_prev, h0, w_in, w_out, a_conv_w, a_log, a_dt_bias,
             a_norm, b_conv_w, b_conv_b, b_ga_w, b_ga_b, b_gx_w, b_gx_b, b_lambda, chunk):
    f32 = jnp.float32
    bsz, t, _ = h.shape
    qkv, z, beta_l, alpha_l, xb, gb = jnp.split(h @ w_in, AB_SPLITS, axis=-1)
    qkv, conv_a_new = causal_conv(qkv, conv_a_prev, a_conv_w)
    q, k, v = jnp.split(jax.nn.silu(qkv), 3, axis=-1)
    q = l2norm(q.reshape(bsz, t, A_HEADS, A_DK)) * A_DK ** -0.5
    k = l2norm(k.reshape(bsz, t, A_HEADS, A_DK))
    v = v.reshape(bsz, t, A_HEADS, A_DV)
    beta = jax.nn.sigmoid(beta_l.astype(f32))
    g = -jnp.exp(a_log.astype(f32)) * jax.nn.softplus(alpha_l.astype(f32) + a_dt_bias.astype(f32))
    o, s_new = gated_delta_rule(q, k, v, g, beta, s0, chunk)
    o = rmsnorm(o, a_norm).astype(h.dtype) * jax.nn.silu(z.reshape(bsz, t, A_HEADS, A_DV))
    xc, conv_b_new = causal_conv(xb, conv_b_prev, b_conv_w)
    xc = xc + b_conv_b
    xblk = xc.reshape(bsz, t, B_BLOCKS, B_BLOCK)
    r = jax.nn.sigmoid(jnp.einsum('btni,nij->btnj', xblk, b_ga_w).reshape(bsz, t, B_WIDTH) + b_ga_b)
    i = jax.nn.sigmoid(jnp.einsum('btni,nij->btnj', xblk, b_gx_w).reshape(bsz, t, B_WIDTH) + b_gx_b)
    log_a = -RG_C * r.astype(f32) * jax.nn.softplus(-b_lambda.astype(f32))
    a = jnp.exp(log_a)
    u = jnp.sqrt(-jnp.expm1(2.0 * log_a)) * (i * xc).astype(f32)
    u = u.at[:, 0].add(a[:, 0] * h0.astype(f32))
    _, hs = lax.associative_scan(lru_combine, (a, u), axis=1)
    ob = hs.astype(h.dtype) * jax.nn.gelu(gb)
    out = jnp.concatenate([o.reshape(bsz, t, A_WIDTH), ob], axis=-1) @ w_out
    return out, conv_a_new, s_new.astype(h.dtype), conv_b_new, hs[:, -1].astype(h.dtype)


def diff_attn_block(q, k, v, q_pos, k_pos, lam, lam_init, norm_g):
    s = jnp.einsum('bqhcd,bkhcd->bhcqk', q.astype(jnp.float32), k.astype(jnp.float32)) * C_DH ** -0.5
    mask = (k_pos[None, :] // CHUNK) <= (q_pos[:, None] // CHUNK)
    p = jax.nn.softmax(jnp.where(mask, s, -jnp.inf), axis=-1)
    p = p[:, :, 0] - lam * p[:, :, 1]
    o = jnp.einsum('bhqk,bkhe->bqhe', p, v.astype(jnp.float32))
    return rmsnorm(o, norm_g) * (1.0 - lam_init)


def mixer_c(h, k_past, v_past, q_pos, k_pos, w_qkv, w_out, lq1, lk1, lq2, lk2, norm_g, lam_init):
    f32 = jnp.float32
    bsz, t, _ = h.shape
    q, k, v = jnp.split(h @ w_qkv, 3, axis=-1)
    q = q.reshape(bsz, t, C_HEADS, 2, C_DH)
    k = k.reshape(bsz, t, C_HEADS, 2, C_DH)
    v = v.reshape(bsz, t, C_HEADS, C_VD)
    lam = (jnp.exp(jnp.sum(lq1.astype(f32) * lk1.astype(f32)))
           - jnp.exp(jnp.sum(lq2.astype(f32) * lk2.astype(f32))) + lam_init)
    if k_past is None:
        outs = []
        for qs in range(0, t, Q_BLOCK):
            qe = min(qs + Q_BLOCK, t)
            outs.append(diff_attn_block(q[:, qs:qe], k[:, :qe], v[:, :qe], q_pos[qs:qe], k_pos[:qe],
                                        lam, lam_init, norm_g))
        o = jnp.concatenate(outs, axis=1)
    else:
        k_all = jnp.concatenate([k_past.astype(h.dtype), k], axis=1)
        v_all = jnp.concatenate([v_past.astype(h.dtype), v], axis=1)
        o = diff_attn_block(q, k_all, v_all, q_pos, k_pos, lam, lam_init, norm_g)
    out = o.reshape(bsz, t, C_WIDTH).astype(h.dtype) @ w_out
    return out, k, v


def peer(h, w_query, sub_keys, u_emb, v_emb):
    f32 = jnp.float32
    bsz, t, d = h.shape
    n_tok = bsz * t
    pad = (-n_tok) % PEER_BLOCK
    hb = jnp.pad(h.reshape(n_tok, d), ((0, pad), (0, 0))).reshape(-1, PEER_BLOCK, d)

    def block(xb):
        qry = (xb @ w_query).astype(f32).reshape(PEER_BLOCK, PEER_HEADS, 2, PEER_DQ // 2)
        sc = jnp.einsum('thpd,pnd->thpn', qry, sub_keys.astype(f32))
        s1, i1 = lax.top_k(sc[:, :, 0], PEER_TOPK)
        s2, i2 = lax.top_k(sc[:, :, 1], PEER_TOPK)
        cand_s = (s1[..., :, None] + s2[..., None, :]).reshape(PEER_BLOCK, PEER_HEADS, PEER_TOPK * PEER_TOPK)
        cand_i = (i1[..., :, None] * N_KEYS + i2[..., None, :]).reshape(PEER_BLOCK, PEER_HEADS, PEER_TOPK * PEER_TOPK)
        top_s, sel = lax.top_k(cand_s, PEER_TOPK)
        idx = jnp.take_along_axis(cand_i, sel, axis=-1)
        gate = jax.nn.softmax(top_s, axis=-1)
        act = jax.nn.gelu(jnp.einsum('thed,td->the', u_emb[idx], xb).astype(f32))
        return jnp.einsum('the,thed->td', (gate * act).astype(xb.dtype), v_emb[idx])

    return lax.map(block, hb).reshape(-1, d)[:n_tok].reshape(bsz, t, d)


def trunk(x, st_a_conv, st_a_delta, st_b_conv, st_b_h, c_k, c_v, p, past):
    t = x.shape[1]
    chunk = min(CHUNK, t)
    q_pos = past + jnp.arange(t)
    k_pos = jnp.arange(past + t)
    a_conv, a_delta, b_conv, b_h, new_k, new_v = [], [], [], [], [], []
    for l in range(DEPTH):
        j = l // 2
        h = rmsnorm(x, p['norm_mix'][l])
        if l % 2 == 0:
            out, ca, sd, cb, hb = mixer_ab(
                h, st_a_conv[j], st_a_delta[j], st_b_conv[j], st_b_h[j],
                p['ab_w_in'][j], p['ab_w_out'][j], p['a_conv_w'][j], p['a_log'][j], p['a_dt_bias'][j],
                p['a_norm'][j], p['b_conv_w'][j], p['b_conv_b'][j], p['b_gate_a_w'][j], p['b_gate_a_b'][j],
                p['b_gate_x_w'][j], p['b_gate_x_b'][j], p['b_lambda'][j], chunk)
            a_conv.append(ca)
            a_delta.append(sd)
            b_conv.append(cb)
            b_h.append(hb)
        else:
            lam_init = 0.8 - 0.6 * math.exp(-0.3 * l)
            out, kn, vn = mixer_c(
                h, None if c_k is None else c_k[j], None if c_v is None else c_v[j], q_pos, k_pos,
                p['c_w_qkv'][j], p['c_w_out'][j], p['c_lambda_q1'][j], p['c_lambda_k1'][j],
                p['c_lambda_q2'][j], p['c_lambda_k2'][j], p['c_norm'][j], lam_init)
            new_k.append(kn)
            new_v.append(vn)
        x = x + out
        x = x + peer(rmsnorm(x, p['norm_ffn'][l]), p['peer_w_query'][l], p['peer_sub_keys'][l],
                     p['peer_u'][l], p['peer_v'][l])
    y = rmsnorm(x, p['norm_final'])
    return (y, jnp.stack(a_conv), jnp.stack(a_delta), jnp.stack(b_conv), jnp.stack(b_h),
            jnp.stack(new_k), jnp.stack(new_v))


def setup_inputs(seed: int = 0) -> dict:
    key = jax.random.key(seed)
    ks = jax.random.split(key, 40)
    f32 = jnp.float32
    d = D_MODEL

    def nrm(i, shape, scale):
        return jax.random.normal(ks[i], shape, f32) * scale

    def gain(i, shape):
        return 1.0 + 0.02 * jax.random.normal(ks[i], shape, f32)

    a_log = jnp.log(jax.random.uniform(ks[9], (N_EVEN, A_HEADS), f32, 1.0, 16.0))
    dt = jnp.exp(jax.random.uniform(ks[10], (N_EVEN, A_HEADS), f32, math.log(1e-3), math.log(1e-1)))
    a_dt_bias = dt + jnp.log(-jnp.expm1(-dt))
    ar = jax.random.uniform(ks[18], (N_EVEN, B_WIDTH), f32, 0.9, 0.999) ** (1.0 / RG_C)
    b_lambda = jnp.log(ar) - jnp.log1p(-ar)
    return {
        'x_prompt': nrm(0, (BATCH, SEQ, d), 1.0),
        'x_sample': nrm(1, (DEC_BATCH, DEC_SEQ, d), 1.0),
        'state_a_conv': nrm(2, (N_EVEN, DEC_BATCH, CONV_W - 1, 3 * A_WIDTH), 1.0),
        'state_a_delta': nrm(3, (N_EVEN, DEC_BATCH, A_HEADS, A_DK, A_DV), 0.1),
        'state_b_conv': nrm(4, (N_EVEN, DEC_BATCH, CONV_W - 1, B_WIDTH), 1.0),
        'state_b_rglru': nrm(5, (N_EVEN, DEC_BATCH, B_WIDTH), 0.5),
        'cache_c_k': nrm(6, (N_ODD, DEC_BATCH, PAST_LEN, C_HEADS, 2, C_DH), 1.0),
        'cache_c_v': nrm(7, (N_ODD, DEC_BATCH, PAST_LEN, C_HEADS, C_VD), 1.0),
        'norm_mix': gain(8, (DEPTH, d)),
        'norm_ffn': gain(11, (DEPTH, d)),
        'norm_final': gain(12, (d,)),
        'ab_w_in': nrm(13, (N_EVEN, d, AB_IN), d ** -0.5),
        'ab_w_out': nrm(14, (N_EVEN, A_WIDTH + B_WIDTH, d), (A_WIDTH + B_WIDTH) ** -0.5),
        'a_conv_w': nrm(15, (N_EVEN, CONV_W, 3 * A_WIDTH), CONV_W ** -0.5),
        'a_log': a_log,
        'a_dt_bias': a_dt_bias,
        'a_norm': gain(16, (N_EVEN, A_DV)),
        'b_conv_w': nrm(17, (N_EVEN, CONV_W, B_WIDTH), CONV_W ** -0.5),
        'b_conv_b': nrm(19, (N_EVEN, B_WIDTH), 0.02),
        'b_gate_a_w': nrm(20, (N_EVEN, B_BLOCKS, B_BLOCK, B_BLOCK), B_BLOCK ** -0.5),
        'b_gate_a_b': nrm(21, (N_EVEN, B_WIDTH), 0.02),
        'b_gate_x_w': nrm(22, (N_EVEN, B_BLOCKS, B_BLOCK, B_BLOCK), B_BLOCK ** -0.5),
        'b_gate_x_b': nrm(23, (N_EVEN, B_WIDTH), 0.02),
        'b_lambda': b_lambda,
        'c_w_qkv': nrm(24, (N_ODD, d, 3 * C_WIDTH), d ** -0.5),
        'c_w_out': nrm(25, (N_ODD, C_WIDTH, d), C_WIDTH ** -0.5),
        'c_lambda_q1': nrm(26, (N_ODD, C_DH), 0.1),
        'c_lambda_k1': nrm(27, (N_ODD, C_DH), 0.1),
        'c_lambda_q2': nrm(28, (N_ODD, C_DH), 0.1),
        'c_lambda_k2': nrm(29, (N_ODD, C_DH), 0.1),
        'c_norm': gain(30, (N_ODD, C_VD)),
        'peer_w_query': nrm(31, (DEPTH, d, PEER_HEADS * PEER_DQ), d ** -0.5),
        'peer_sub_keys': nrm(32, (DEPTH, 2, N_KEYS, PEER_DQ // 2), (PEER_DQ // 2) ** -0.5),
        'peer_u': nrm(33, (DEPTH, N_EXPERTS, d), d ** -0.5),
        'peer_v': nrm(34, (DEPTH, N_EXPERTS, d), PEER_HEADS ** -0.5),
    }


def reference(x_prompt, x_sample, state_a_conv, state_a_delta, state_b_conv, state_b_rglru,
              cache_c_k, cache_c_v, norm_mix, norm_ffn, norm_final, ab_w_in, ab_w_out, a_conv_w,
              a_log, a_dt_bias, a_norm, b_conv_w, b_conv_b, b_gate_a_w, b_gate_a_b, b_gate_x_w,
              b_gate_x_b, b_lambda, c_w_qkv, c_w_out, c_lambda_q1, c_lambda_k1, c_lambda_q2,
              c_lambda_k2, c_norm, peer_w_query, peer_sub_keys, peer_u, peer_v):
    p = dict(norm_mix=norm_mix, norm_ffn=norm_ffn, norm_final=norm_final, ab_w_in=ab_w_in,
             ab_w_out=ab_w_out, a_conv_w=a_conv_w, a_log=a_log, a_dt_bias=a_dt_bias, a_norm=a_norm,
             b_conv_w=b_conv_w, b_conv_b=b_conv_b, b_gate_a_w=b_gate_a_w, b_gate_a_b=b_gate_a_b,
             b_gate_x_w=b_gate_x_w, b_gate_x_b=b_gate_x_b, b_lambda=b_lambda, c_w_qkv=c_w_qkv,
             c_w_out=c_w_out, c_lambda_q1=c_lambda_q1, c_lambda_k1=c_lambda_k1,
             c_lambda_q2=c_lambda_q2, c_lambda_k2=c_lambda_k2, c_norm=c_norm,
             peer_w_query=peer_w_query, peer_sub_keys=peer_sub_keys, peer_u=peer_u, peer_v=peer_v)
    bp = x_prompt.shape[0]
    dt = x_prompt.dtype
    z_a_conv = jnp.zeros((N_EVEN, bp, CONV_W - 1, 3 * A_WIDTH), dt)
    z_a_delta = jnp.zeros((N_EVEN, bp, A_HEADS, A_DK, A_DV), dt)
    z_b_conv = jnp.zeros((N_EVEN, bp, CONV_W - 1, B_WIDTH), dt)
    z_b_h = jnp.zeros((N_EVEN, bp, B_WIDTH), dt)
    (y_prompt, pa_conv, pa_delta, pb_conv, pb_h, pc_k, pc_v) = trunk(
        x_prompt, z_a_conv, z_a_delta, z_b_conv, z_b_h, None, None, p, 0)
    (y_sample, sa_conv, sa_delta, sb_conv, sb_h, sc_k, sc_v) = trunk(
        x_sample, state_a_conv, state_a_delta, state_b_conv, state_b_rglru, cache_c_k, cache_c_v,
        p, cache_c_k.shape[2])
    return (y_prompt, y_sample, pa_conv, pa_delta, pb_conv, pb_h, pc_k, pc_v,
            sa_conv, sa_delta, sb_conv, sb_h, sc_k, sc_v)
```

```python
import functools
import math

import jax
import jax.numpy as jnp
from jax import lax
from jax.experimental import pallas as pl
from jax.experimental.pallas import tpu as pltpu

F32 = jnp.float32
BF16 = jnp.bfloat16
HI = lax.Precision.HIGHEST

EPS = 1e-6
CHUNK = 64
CONV_W = 4
A_HEADS = 4
A_DK = 128
A_WIDTH = A_HEADS * A_DK
B_WIDTH = 512
B_BLOCKS = 8
RG_C = 8.0
C_HEADS = 8
C_DH = 64
C_VD = 2 * C_DH
C_WIDTH = C_HEADS * C_VD
PEER_HEADS = 8
N_KEYS = 128
PEER_TOPK = 16
PEER_DQ = 256
LANES = 128
VMEM_LIMIT = 48 * 1024 * 1024


def _cparams(sem):
    return pltpu.CompilerParams(dimension_semantics=sem, vmem_limit_bytes=VMEM_LIMIT)


def _dot(a, b, prec=None):
    return jnp.dot(a, b, precision=prec, preferred_element_type=F32)


def _dot_nt(a, b, prec=None):
    return lax.dot_general(a, b, (((1,), (1,)), ((), ())), precision=prec, preferred_element_type=F32)


def _rms(x, g):
    return x * lax.rsqrt(jnp.mean(x * x, axis=-1, keepdims=True) + EPS) * g


def _softplus(x):
    return jnp.maximum(x, 0.0) + jnp.log1p(jnp.exp(-jnp.abs(x)))


def _row_tile(n, pref):
    t = min(n, pref)
    while n % t:
        t //= 2
    return t


def _norm_matmul_kernel(x_ref, g_ref, w_ref, o_ref):
    y = _rms(x_ref[...], g_ref[...])
    o_ref[...] = _dot(y.astype(BF16), w_ref[...])


def norm_matmul(x, g, w):
    n, d = x.shape
    m = w.shape[1]
    tm = _row_tile(n, 256)
    return pl.pallas_call(
        _norm_matmul_kernel,
        grid=(n // tm,),
        in_specs=[
            pl.BlockSpec((tm, d), lambda i: (i, 0)),
            pl.BlockSpec((1, d), lambda i: (0, 0)),
            pl.BlockSpec((d, m), lambda i: (0, 0)),
        ],
        out_specs=pl.BlockSpec((tm, m), lambda i: (i, 0)),
        out_shape=jax.ShapeDtypeStruct((n, m), F32),
        compiler_params=_cparams(("parallel",)),
        name="norm_matmul",
    )(x, g.reshape(1, d), w)


def _matmul_res_kernel(*refs, n_in):
    ys = refs[:n_in]
    ws = refs[n_in:2 * n_in]
    res_ref = refs[2 * n_in]
    o_ref = refs[2 * n_in + 1]
    acc = res_ref[...]
    for y_ref, w_ref in zip(ys, ws):
        acc = acc + _dot(y_ref[...].astype(BF16), w_ref[...])
    o_ref[...] = acc


def matmul_res(ys, ws, res):
    n, d = res.shape
    tm = _row_tile(n, 512)
    n_in = len(ys)
    in_specs = [pl.BlockSpec((tm, y.shape[1]), lambda i: (i, 0)) for y in ys]
    in_specs += [pl.BlockSpec(w.shape, lambda i: (0, 0)) for w in ws]
    in_specs += [pl.BlockSpec((tm, d), lambda i: (i, 0))]
    return pl.pallas_call(
        functools.partial(_matmul_res_kernel, n_in=n_in),
        grid=(n // tm,),
        in_specs=in_specs,
        out_specs=pl.BlockSpec((tm, d), lambda i: (i, 0)),
        out_shape=jax.ShapeDtypeStruct((n, d), F32),
        compiler_params=_cparams(("parallel",)),
        name="matmul_res",
    )(*ys, *ws, res)


def _gdn_kernel(qkv_ref, z_ref, ba_ref, cprev_ref, s0_ref, cw_ref, prm_ref, an_ref,
                o_ref, sout_ref, xp_ref, s_ref, *, chunk):
    c = chunk
    nh = A_HEADS
    r = nh * c
    step = pl.program_id(1)

    @pl.when(step == 0)
    def _():
        xp_ref[5:8, :] = cprev_ref[0]
        for h in range(nh):
            s_ref[:, h * A_DK:(h + 1) * A_DK] = s0_ref[0, h]

    xp_ref[8:8 + c, :] = qkv_ref[0]
    conv = xp_ref[5:5 + c, :] * cw_ref[0:1, :]
    for j in range(1, CONV_W):
        conv = conv + xp_ref[5 + j:5 + j + c, :] * cw_ref[j:j + 1, :]
    xp_ref[5:8, :] = xp_ref[5 + c:8 + c, :]
    act = conv * jax.nn.sigmoid(conv)

    def stack(off):
        return jnp.concatenate([act[:, off + h * A_DK: off + (h + 1) * A_DK] for h in range(nh)], axis=0)

    qs, ks, vs = stack(0), stack(A_WIDTH), stack(2 * A_WIDTH)
    qs = qs * lax.rsqrt(jnp.sum(qs * qs, axis=-1, keepdims=True) + EPS) * (A_DK ** -0.5)
    ks = ks * lax.rsqrt(jnp.sum(ks * ks, axis=-1, keepdims=True) + EPS)

    ba = ba_ref[0]
    beta_full = jax.nn.sigmoid(ba)
    g_full = -jnp.exp(prm_ref[0:1, :]) * _softplus(ba + prm_ref[1:2, :])
    beta_s = jnp.concatenate([beta_full[:, h:h + 1] for h in range(nh)], axis=0)
    g_s = jnp.concatenate([g_full[:, nh + h:nh + h + 1] for h in range(nh)], axis=0)

    row = lax.broadcasted_iota(jnp.int32, (r, r), 0)
    col = lax.broadcasted_iota(jnp.int32, (r, r), 1)
    same = (row // c) == (col // c)
    m_incl = same & (col <= row)
    m_strict = same & (col < row)

    gcs = _dot(m_incl.astype(F32), jnp.broadcast_to(g_s, (r, LANES)), HI)
    lane0 = (lax.broadcasted_iota(jnp.int32, (r, LANES), 1) == 0).astype(F32)
    gcs_row = _dot_nt(lane0, gcs, HI)
    gc = gcs[:, 0:1]
    decay = jnp.exp(jnp.where(m_incl, gc - gcs_row, -jnp.inf))

    kb = ks * beta_s
    a_mat = jnp.where(m_strict, _dot_nt(kb, ks, HI), 0.0) * decay
    eye = (row == col).astype(F32)
    p = -a_mat
    t_inv = eye + p
    for _ in range(int(math.log2(c)) - 1):
        p = _dot(p, p, HI)
        t_inv = t_inv + _dot(t_inv, p, HI)
    uw = _dot(t_inv, jnp.concatenate([vs * beta_s, kb * jnp.exp(gc)], axis=1), HI)
    u, w = uw[:, :A_DK], uw[:, A_DK:]
    qk = jnp.where(m_incl, _dot_nt(qs, ks, HI), 0.0) * decay

    def diag_blocks(x):
        return jnp.concatenate([x[h * c:(h + 1) * c, h * A_DK:(h + 1) * A_DK] for h in range(nh)], axis=0)

    s_cat = s_ref[...]
    wq_s = _dot(jnp.concatenate([w, qs * jnp.exp(gc)], axis=0), s_cat, HI)
    v_new = u - diag_blocks(wq_s[:r])
    o_s = diag_blocks(wq_s[r:]) + _dot(qk, v_new, HI)

    g_last = jnp.concatenate(
        [jnp.broadcast_to(gcs[(h + 1) * c - 1:(h + 1) * c, :], (c, LANES)) for h in range(nh)], axis=0)
    kdec = ks * jnp.exp(g_last - gcs)
    zero = jnp.zeros((c, A_DK), F32)
    v_exp = jnp.concatenate(
        [jnp.concatenate([v_new[h * c:(h + 1) * c] if hh == h else zero for hh in range(nh)], axis=1)
         for h in range(nh)], axis=0)
    s_scale = jnp.concatenate(
        [jnp.exp(jnp.broadcast_to(gcs[(h + 1) * c - 1:(h + 1) * c, :], (A_DK, LANES))) for h in range(nh)], axis=1)
    s_new = s_cat * s_scale + _dot(kdec.T, v_exp, HI)
    s_ref[...] = s_new

    o_n = _rms(o_s, an_ref[...])
    o_full = jnp.concatenate([o_n[h * c:(h + 1) * c] for h in range(nh)], axis=1)
    zz = z_ref[0]
    o_ref[0] = o_full * (zz * jax.nn.sigmoid(zz))

    @pl.when(step == pl.num_programs(1) - 1)
    def _():
        for h in range(nh):
            sout_ref[0, h] = s_new[:, h * A_DK:(h + 1) * A_DK]


def gdn(proj, conv_prev, s0, conv_w, a_log, dt_bias, a_norm, chunk):
    b, t, _ = proj.shape
    nsteps = t // chunk
    prm = jnp.zeros((2, LANES), F32)
    prm = prm.at[0, A_HEADS:2 * A_HEADS].set(a_log).at[1, A_HEADS:2 * A_HEADS].set(dt_bias)
    qkv_w = 3 * A_WIDTH
    return pl.pallas_call(
        functools.partial(_gdn_kernel, chunk=chunk),
        grid=(b, nsteps),
        in_specs=[
            pl.BlockSpec((1, chunk, qkv_w), lambda i, j: (i, j, 0)),
            pl.BlockSpec((1, chunk, A_WIDTH), lambda i, j: (i, j, qkv_w // A_WIDTH)),
            pl.BlockSpec((1, chunk, LANES), lambda i, j: (i, j, (qkv_w + 3 * A_WIDTH) // LANES)),
            pl.BlockSpec((1, CONV_W - 1, qkv_w), lambda i, j: (i, 0, 0)),
            pl.BlockSpec((1, A_HEADS, A_DK, A_DK), lambda i, j: (i, 0, 0, 0)),
            pl.BlockSpec((CONV_W, qkv_w), lambda i, j: (0, 0)),
            pl.BlockSpec((2, LANES), lambda i, j: (0, 0)),
            pl.BlockSpec((1, A_DK), lambda i, j: (0, 0)),
        ],
        out_specs=[
            pl.BlockSpec((1, chunk, A_WIDTH), lambda i, j: (i, j, 0)),
            pl.BlockSpec((1, A_HEADS, A_DK, A_DK), lambda i, j: (i, 0, 0, 0)),
        ],
        out_shape=[
            jax.ShapeDtypeStruct((b, t, A_WIDTH), F32),
            jax.ShapeDtypeStruct((b, A_HEADS, A_DK, A_DK), F32),
        ],
        scratch_shapes=[
            pltpu.VMEM((chunk + 8, qkv_w), F32),
            pltpu.VMEM((A_DK, A_HEADS * A_DK), F32),
        ],
        compiler_params=_cparams(("parallel", "arbitrary")),
        name="gdn",
    )(proj, proj, proj, conv_prev, s0, conv_w, prm, a_norm.reshape(1, A_DK))


def _lru_kernel(xb_ref, gb_ref, cprev_ref, h0_ref, cw_ref, cb_ref, wa_ref, ba_ref, wx_ref, bx_ref, lam_ref,
                ob_ref, hl_ref, xp_ref, a_ref, u_ref, hs_ref, h_ref, *, tb):
    step = pl.program_id(1)

    @pl.when(step == 0)
    def _():
        xp_ref[5:8, :] = cprev_ref[0]
        h_ref[...] = h0_ref[0]

    xp_ref[8:8 + tb, :] = xb_ref[0]
    xc = xp_ref[5:5 + tb, :] * cw_ref[0:1, :]
    for j in range(1, CONV_W):
        xc = xc + xp_ref[5 + j:5 + j + tb, :] * cw_ref[j:j + 1, :]
    xp_ref[5:8, :] = xp_ref[5 + tb:8 + tb, :]
    xc = xc + cb_ref[...]
    xcb = xc.astype(BF16)
    rg = jax.nn.sigmoid(_dot(xcb, wa_ref[...]) + ba_ref[...])
    ig = jax.nn.sigmoid(_dot(xcb, wx_ref[...]) + bx_ref[...])
    log_a = -RG_C * rg * _softplus(-lam_ref[...])
    a_ref[...] = jnp.exp(log_a)
    u_ref[...] = jnp.sqrt(1.0 - jnp.exp(2.0 * log_a)) * (ig * xc)

    def body(i, h):
        h = a_ref[pl.ds(i, 1), :] * h + u_ref[pl.ds(i, 1), :]
        hs_ref[pl.ds(i, 1), :] = h
        return h

    h = lax.fori_loop(0, tb, body, h_ref[...], unroll=8)
    h_ref[...] = h
    ob_ref[0] = hs_ref[...] * jax.nn.gelu(gb_ref[0])

    @pl.when(step == pl.num_programs(1) - 1)
    def _():
        hl_ref[0] = h


def lru(proj, conv_prev, h0, conv_w, conv_b, wa_bd, ba, wx_bd, bx, lam):
    b, t, _ = proj.shape
    tb = _row_tile(t, 256)
    xb_blk = (3 * A_WIDTH + A_WIDTH) // B_WIDTH
    vec = lambda v: v.reshape(1, B_WIDTH)
    const = lambda shape: pl.BlockSpec(shape, lambda i, j: (0,) * len(shape))
    ob, hl = pl.pallas_call(
        functools.partial(_lru_kernel, tb=tb),
        grid=(b, t // tb),
        in_specs=[
            pl.BlockSpec((1, tb, B_WIDTH), lambda i, j: (i, j, xb_blk)),
            pl.BlockSpec((1, tb, B_WIDTH), lambda i, j: (i, j, xb_blk + 1)),
            pl.BlockSpec((1, CONV_W - 1, B_WIDTH), lambda i, j: (i, 0, 0)),
            pl.BlockSpec((1, 1, B_WIDTH), lambda i, j: (i, 0, 0)),
            const((CONV_W, B_WIDTH)), const((1, B_WIDTH)),
            const((B_WIDTH, B_WIDTH)), const((1, B_WIDTH)),
            const((B_WIDTH, B_WIDTH)), const((1, B_WIDTH)),
            const((1, B_WIDTH)),
        ],
        out_specs=[
            pl.BlockSpec((1, tb, B_WIDTH), lambda i, j: (i, j, 0)),
            pl.BlockSpec((1, 1, B_WIDTH), lambda i, j: (i, 0, 0)),
        ],
        out_shape=[
            jax.ShapeDtypeStruct((b, t, B_WIDTH), F32),
            jax.ShapeDtypeStruct((b, 1, B_WIDTH), F32),
        ],
        scratch_shapes=[
            pltpu.VMEM((tb + 8, B_WIDTH), F32),
            pltpu.VMEM((tb, B_WIDTH), F32),
            pltpu.VMEM((tb, B_WIDTH), F32),
            pltpu.VMEM((tb, B_WIDTH), F32),
            pltpu.VMEM((1, B_WIDTH), F32),
        ],
        compiler_params=_cparams(("parallel", "arbitrary")),
        name="lru",
    )(proj, proj, conv_prev, h0.reshape(b, 1, B_WIDTH), conv_w, vec(conv_b), wa_bd, vec(ba), wx_bd, vec(bx), vec(lam))
    return ob, hl.reshape(b, B_WIDTH)


def _attn_kernel(q_ref, k_ref, v_ref, lam_ref, ng_ref, o_ref, *, tq, past, lam_init):
    qi = pl.program_id(2)
    q = q_ref[0]
    k = k_ref[0].astype(BF16)
    v = v_ref[0].astype(BF16)
    tk = k.shape[0]
    lane = lax.broadcasted_iota(jnp.int32, q.shape, 1)
    q1 = jnp.where(lane < C_DH, q, 0.0).astype(BF16)
    q2 = jnp.where(lane >= C_DH, q, 0.0).astype(BF16)
    lv = lam_ref[...]
    lam = (jnp.exp(jnp.sum(lv[0:1] * lv[1:2], axis=-1, keepdims=True))
           - jnp.exp(jnp.sum(lv[2:3] * lv[3:4], axis=-1, keepdims=True)) + lam_init)
    q_pos = past + qi * tq + lax.broadcasted_iota(jnp.int32, (tq, tk), 0)
    k_pos = lax.broadcasted_iota(jnp.int32, (tq, tk), 1)
    mask = (k_pos // CHUNK) <= (q_pos // CHUNK)

    def probs(qc):
        s = jnp.where(mask, _dot_nt(qc, k) * (C_DH ** -0.5), -jnp.inf)
        e = jnp.exp(s - jnp.max(s, axis=-1, keepdims=True))
        return e / jnp.sum(e, axis=-1, keepdims=True)

    pm = probs(q1) - lam * probs(q2)
    o = _dot(pm.astype(BF16), v)
    o_ref[0] = _rms(o, ng_ref[...]) * (1.0 - lam_init)


def diff_attn(q_src, kv_src, q_col0, k_col0, v_col0, lam_vecs, norm_g, past, lam_init):
    b, t_q, _ = q_src.shape
    t_k = kv_src.shape[1]
    tq = _row_tile(t_q, 256)
    return pl.pallas_call(
        functools.partial(_attn_kernel, tq=tq, past=past, lam_init=lam_init),
        grid=(b, C_HEADS, t_q // tq),
        in_specs=[
            pl.BlockSpec((1, tq, C_VD), lambda i, h, j: (i, j, q_col0 + h)),
            pl.BlockSpec((1, t_k, C_VD), lambda i, h, j: (i, 0, k_col0 + h)),
            pl.BlockSpec((1, t_k, C_VD), lambda i, h, j: (i, 0, v_col0 + h)),
            pl.BlockSpec((4, C_DH), lambda i, h, j: (0, 0)),
            pl.BlockSpec((1, C_VD), lambda i, h, j: (0, 0)),
        ],
        out_specs=pl.BlockSpec((1, tq, C_VD), lambda i, h, j: (i, j, h)),
        out_shape=jax.ShapeDtypeStruct((b, t_q, C_WIDTH), F32),
        compiler_params=_cparams(("parallel", "parallel", "arbitrary")),
        name="diff_attn",
    )(q_src, kv_src, kv_src, lam_vecs, norm_g.reshape(1, C_VD))


def _topk_rows(vals, payload, k, fill):
    n_rows = vals.shape[0]
    rio = lax.broadcasted_iota(jnp.int32, vals.shape, 0)
    out_v, out_p = [], []
    for _ in range(k):
        m = jnp.max(vals, axis=0, keepdims=True)
        am = jnp.min(jnp.where(vals == m, rio, n_rows), axis=0, keepdims=True)
        hit = rio == am
        out_v.append(m)
        out_p.append(jnp.max(jnp.where(hit, payload, fill), axis=0, keepdims=True))
        vals = jnp.where(hit, -jnp.inf, vals)
    return jnp.concatenate(out_v, axis=0), jnp.concatenate(out_p, axis=0)


def _peer_topk_kernel(q_ref, keys_ref, idx_ref, gate_ref, idx_t, gate_t):
    tt = q_ref.shape[0]
    kio = lax.broadcasted_iota(jnp.int32, (N_KEYS, tt), 0)
    half = PEER_DQ // 2

    def head(h, carry):
        tops = []
        for p in range(2):
            off = pl.multiple_of(h * PEER_DQ + p * half, half)
            q = q_ref[:, pl.ds(off, half)]
            sc = _dot_nt(keys_ref[p], q, HI)
            tops.append(_topk_rows(sc, kio, PEER_TOPK, -1))
        (s1, i1), (s2, i2) = tops
        cand_s = jnp.concatenate([s1[i:i + 1] + s2 for i in range(PEER_TOPK)], axis=0)
        cand_e = jnp.concatenate([i1[i:i + 1] * N_KEYS + i2 for i in range(PEER_TOPK)], axis=0)
        top_s, top_e = _topk_rows(cand_s, cand_e, PEER_TOPK, -1)
        e = jnp.exp(top_s - top_s[0:1])
        gate = e / jnp.sum(e, axis=0, keepdims=True)
        r0 = pl.multiple_of(h * PEER_TOPK, PEER_TOPK)
        idx_t[pl.ds(r0, PEER_TOPK), :] = top_e
        gate_t[pl.ds(r0, PEER_TOPK), :] = gate
        return carry

    lax.fori_loop(0, PEER_HEADS, head, 0)
    idx_ref[...] = idx_t[...].T
    gate_ref[...] = gate_t[...].T


def peer_topk(qry, sub_keys):
    n = qry.shape[0]
    tt = _row_tile(n, 256)
    nsel = PEER_HEADS * PEER_TOPK
    return pl.pallas_call(
        _peer_topk_kernel,
        grid=(n // tt,),
        in_specs=[
            pl.BlockSpec((tt, PEER_HEADS * PEER_DQ), lambda i: (i, 0)),
            pl.BlockSpec((2, N_KEYS, PEER_DQ // 2), lambda i: (0, 0, 0)),
        ],
        out_specs=[
            pl.BlockSpec((tt, nsel), lambda i: (i, 0)),
            pl.BlockSpec((tt, nsel), lambda i: (i, 0)),
        ],
        out_shape=[
            jax.ShapeDtypeStruct((n, nsel), jnp.int32),
            jax.ShapeDtypeStruct((n, nsel), F32),
        ],
        scratch_shapes=[pltpu.VMEM((nsel, tt), jnp.int32), pltpu.VMEM((nsel, tt), F32)],
        compiler_params=_cparams(("parallel",)),
        name="peer_topk",
    )(qry, sub_keys)


PEER_GROUP = 8


def _peer_expert_kernel(idx_ref, idxn_ref, x_ref, gate_ref, g_ref, gf_ref, uv_hbm, o_ref, gbuf, sem,
                        *, final_norm):
    g = PEER_GROUP
    nsel = PEER_HEADS * PEER_TOPK
    d = x_ref.shape[1]
    step = pl.program_id(0)
    nsteps = pl.num_programs(0)
    slot = step % 2

    def row_copy(src_row, slot_, dst_row):
        return pltpu.make_async_copy(uv_hbm.at[pl.ds(src_row, 1)], gbuf.at[slot_, pl.ds(dst_row, 1)], sem.at[slot_])

    def issue(iref, slot_):
        def body(t, c):
            for r in range(nsel):
                row_copy(iref[t, r], slot_, t * nsel + r).start()
            return c
        lax.fori_loop(0, g, body, 0)

    @pl.when(step == 0)
    def _():
        issue(idx_ref, 0)

    @pl.when(step + 1 < nsteps)
    def _():
        issue(idxn_ref, 1 - slot)

    def wait_body(t, c):
        for r in range(nsel):
            row_copy(0, slot, t * nsel + r).wait()
        return c
    lax.fori_loop(0, g, wait_body, 0)

    x = x_ref[...]
    xnb = _rms(x, g_ref[...]).astype(BF16)
    rio = lax.broadcasted_iota(jnp.int32, (g, nsel), 0)
    act = jnp.zeros((g, nsel), F32)
    for t in range(g):
        gu = gbuf[slot, t * nsel:(t + 1) * nsel, 0:d].astype(BF16)
        act = jnp.where(rio == t, _dot_nt(xnb, gu), act)
    w = gate_ref[...] * jax.nn.gelu(act)
    out = x
    for t in range(g):
        gv = gbuf[slot, t * nsel:(t + 1) * nsel, d:2 * d].astype(BF16)
        out = out + _dot(jnp.where(rio == t, w, 0.0).astype(BF16), gv)
    if final_norm:
        out = _rms(out, gf_ref[...])
    o_ref[...] = out


def peer_experts(x, idx, gate, norm_g, final_g, uv, final_norm):
    n, d = x.shape
    g = PEER_GROUP
    nsel = PEER_HEADS * PEER_TOPK
    nsteps = n // g
    return pl.pallas_call(
        functools.partial(_peer_expert_kernel, final_norm=final_norm),
        grid=(nsteps,),
        in_specs=[
            pl.BlockSpec((g, nsel), lambda i: (i, 0), memory_space=pltpu.SMEM),
            pl.BlockSpec((g, nsel), lambda i: (jnp.minimum(i + 1, nsteps - 1), 0), memory_space=pltpu.SMEM),
            pl.BlockSpec((g, d), lambda i: (i, 0)),
            pl.BlockSpec((g, nsel), lambda i: (i, 0)),
            pl.BlockSpec((1, d), lambda i: (0, 0)),
            pl.BlockSpec((1, d), lambda i: (0, 0)),
            pl.BlockSpec(memory_space=pl.ANY),
        ],
        out_specs=pl.BlockSpec((g, d), lambda i: (i, 0)),
        out_shape=jax.ShapeDtypeStruct((n, d), F32),
        scratch_shapes=[
            pltpu.VMEM((2, g * nsel, 2 * d), F32),
            pltpu.SemaphoreType.DMA((2,)),
        ],
        compiler_params=_cparams(("arbitrary",)),
        name="peer_experts",
    )(idx, idx, x, gate, norm_g.reshape(1, d), final_g.reshape(1, d), uv)


def _block_diag(w):
    nb, bi, bj = w.shape
    out = jnp.zeros((nb * bi, nb * bj), w.dtype)
    for i in range(nb):
        out = out.at[i * bi:(i + 1) * bi, i * bj:(i + 1) * bj].set(w[i])
    return out


def _prep_weights(p):
    depth = p['norm_mix'].shape[0]
    d = p['norm_final'].shape[0]
    w = {}
    a3, a4 = 3 * A_WIDTH, 4 * A_WIDTH
    w['ab_in'], w['ab_out_a'], w['ab_out_b'], w['ga'], w['gx'] = [], [], [], [], []
    for j in range(p['ab_w_in'].shape[0]):
        wi = p['ab_w_in'][j]
        pad = jnp.zeros((d, LANES - 2 * A_HEADS), wi.dtype)
        cols = [wi[:, :a4], wi[:, a4 + 2 * A_HEADS:], wi[:, a4:a4 + 2 * A_HEADS], pad]
        w['ab_in'].append(jnp.concatenate(cols, axis=1).astype(BF16))
        w['ab_out_a'].append(p['ab_w_out'][j][:A_WIDTH].astype(BF16))
        w['ab_out_b'].append(p['ab_w_out'][j][A_WIDTH:].astype(BF16))
        w['ga'].append(_block_diag(p['b_gate_a_w'][j]).astype(BF16))
        w['gx'].append(_block_diag(p['b_gate_x_w'][j]).astype(BF16))
    w['c_qkv'] = [m.astype(BF16) for m in p['c_w_qkv']]
    w['c_out'] = [m.astype(BF16) for m in p['c_w_out']]
    w['c_lam'] = [jnp.stack([p['c_lambda_q1'][j], p['c_lambda_k1'][j], p['c_lambda_q2'][j], p['c_lambda_k2'][j]])
                  for j in range(p['c_w_qkv'].shape[0])]
    w['peer_q'] = [p['peer_w_query'][l].astype(BF16) for l in range(depth)]
    w['peer_uv'] = [jnp.concatenate([p['peer_u'][l], p['peer_v'][l]], axis=1) for l in range(depth)]
    return w


def _peer(x2, l, p, w, final_norm):
    qry = norm_matmul(x2, p['norm_ffn'][l], w['peer_q'][l])
    idx, gate = peer_topk(qry, p['peer_sub_keys'][l])
    return peer_experts(x2, idx, gate, p['norm_ffn'][l], p['norm_final'], w['peer_uv'][l], final_norm)


def _trunk(x, st_a_conv, st_a_delta, st_b_conv, st_b_h, c_k, c_v, p, w, past):
    b, t, d = x.shape
    n = b * t
    chunk = min(CHUNK, t)
    depth = p['norm_mix'].shape[0]
    x2 = x.reshape(n, d)
    a_conv, a_delta, b_conv, b_h, new_k, new_v = [], [], [], [], [], []
    for l in range(depth):
        j = l // 2
        if l % 2 == 0:
            proj = norm_matmul(x2, p['norm_mix'][l], w['ab_in'][j]).reshape(b, t, -1)
            o_a, s_new = gdn(proj, st_a_conv[j], st_a_delta[j], p['a_conv_w'][j], p['a_log'][j],
                             p['a_dt_bias'][j], p['a_norm'][j], chunk)
            o_b, h_last = lru(proj, st_b_conv[j], st_b_h[j], p['b_conv_w'][j], p['b_conv_b'][j], w['ga'][j],
                              p['b_gate_a_b'][j], w['gx'][j], p['b_gate_x_b'][j], p['b_lambda'][j])
            a3 = 3 * A_WIDTH
            xb0 = a3 + A_WIDTH
            a_conv.append(jnp.concatenate([st_a_conv[j], proj[:, :, :a3]], axis=1)[:, -(CONV_W - 1):])
            b_conv.append(jnp.concatenate([st_b_conv[j], proj[:, :, xb0:xb0 + B_WIDTH]], axis=1)[:, -(CONV_W - 1):])
            a_delta.append(s_new)
            b_h.append(h_last)
            x2 = matmul_res([o_a.reshape(n, A_WIDTH), o_b.reshape(n, B_WIDTH)],
                            [w['ab_out_a'][j], w['ab_out_b'][j]], x2)
        else:
            lam_init = 0.8 - 0.6 * math.exp(-0.3 * l)
            qkv = norm_matmul(x2, p['norm_mix'][l], w['c_qkv'][j]).reshape(b, t, 3 * C_WIDTH)
            kn = qkv[:, :, C_WIDTH:2 * C_WIDTH]
            vn = qkv[:, :, 2 * C_WIDTH:]
            if c_k is None:
                o_c = diff_attn(qkv, qkv, 0, C_HEADS, 2 * C_HEADS, w['c_lam'][j], p['c_norm'][j], past, lam_init)
            else:
                k_all = jnp.concatenate([c_k[j].reshape(b, -1, C_WIDTH), kn], axis=1)
                v_all = jnp.concatenate([c_v[j].reshape(b, -1, C_WIDTH), vn], axis=1)
                kv = jnp.concatenate([k_all, v_all], axis=2)
                o_c = diff_attn(qkv, kv, 0, 0, C_HEADS, w['c_lam'][j], p['c_norm'][j], past, lam_init)
            new_k.append(kn.reshape(b, t, C_HEADS, 2, C_DH))
            new_v.append(vn.reshape(b, t, C_HEADS, C_VD))
            x2 = matmul_res([o_c.reshape(n, C_WIDTH)], [w['c_out'][j]], x2)
        x2 = _peer(x2, l, p, w, final_norm=(l == depth - 1))
    return (x2.reshape(b, t, d), jnp.stack(a_conv), jnp.stack(a_delta), jnp.stack(b_conv), jnp.stack(b_h),
            jnp.stack(new_k), jnp.stack(new_v))


def kernel(x_prompt, x_sample, state_a_conv, state_a_delta, state_b_conv, state_b_rglru, cache_c_k, cache_c_v, norm_mix, norm_ffn, norm_final, ab_w_in, ab_w_out, a_conv_w, a_log, a_dt_bias, a_norm, b_conv_w, b_conv_b, b_gate_a_w, b_gate_a_b, b_gate_x_w, b_gate_x_b, b_lambda, c_w_qkv, c_w_out, c_lambda_q1, c_lambda_k1, c_lambda_q2, c_lambda_k2, c_norm, peer_w_query, peer_sub_keys, peer_u, peer_v):
    p = dict(norm_mix=norm_mix, norm_ffn=norm_ffn, norm_final=norm_final, ab_w_in=ab_w_in,
             ab_w_out=ab_w_out, a_conv_w=a_conv_w, a_log=a_log, a_dt_bias=a_dt_bias, a_norm=a_norm,
             b_conv_w=b_conv_w, b_conv_b=b_conv_b, b_gate_a_w=b_gate_a_w, b_gate_a_b=b_gate_a_b,
             b_gate_x_w=b_gate_x_w, b_gate_x_b=b_gate_x_b, b_lambda=b_lambda, c_w_qkv=c_w_qkv,
             c_w_out=c_w_out, c_lambda_q1=c_lambda_q1, c_lambda_k1=c_lambda_k1,
             c_lambda_q2=c_lambda_q2, c_lambda_k2=c_lambda_k2, c_norm=c_norm,
             peer_w_query=peer_w_query, peer_sub_keys=peer_sub_keys, peer_u=peer_u, peer_v=peer_v)
    w = _prep_weights(p)
    bp = x_prompt.shape[0]
    dt = x_prompt.dtype
    n_even = state_a_conv.shape[0]
    z_a_conv = jnp.zeros((n_even, bp) + state_a_conv.shape[2:], dt)
    z_a_delta = jnp.zeros((n_even, bp) + state_a_delta.shape[2:], dt)
    z_b_conv = jnp.zeros((n_even, bp) + state_b_conv.shape[2:], dt)
    z_b_h = jnp.zeros((n_even, bp) + state_b_rglru.shape[2:], dt)
    (y_prompt, pa_conv, pa_delta, pb_conv, pb_h, pc_k, pc_v) = _trunk(
        x_prompt, z_a_conv, z_a_delta, z_b_conv, z_b_h, None, None, p, w, 0)
    (y_sample, sa_conv, sa_delta, sb_conv, sb_h, sc_k, sc_v) = _trunk(
        x_sample, state_a_conv, state_a_delta, state_b_conv, state_b_rglru, cache_c_k, cache_c_v,
        p, w, cache_c_k.shape[2])
    return (y_prompt, y_sample, pa_conv, pa_delta, pb_conv, pb_h, pc_k, pc_v,
            sa_conv, sa_delta, sb_conv, sb_h, sc_k, sc_v)
```

```python
import functools
import math

import jax
import jax.numpy as jnp
from jax import lax
from jax.experimental import pallas as pl
from jax.experimental.pallas import tpu as pltpu

F32 = jnp.float32
BF16 = jnp.bfloat16
HI = lax.Precision.HIGHEST

EPS = 1e-6
CHUNK = 64
CONV_W = 4
A_HEADS = 4
A_DK = 128
A_WIDTH = A_HEADS * A_DK
B_WIDTH = 512
B_BLOCKS = 8
RG_C = 8.0
C_HEADS = 8
C_DH = 64
C_VD = 2 * C_DH
C_WIDTH = C_HEADS * C_VD
PEER_HEADS = 8
N_KEYS = 128
PEER_TOPK = 16
PEER_DQ = 256
LANES = 128
VMEM_LIMIT = 48 * 1024 * 1024


def _cparams(sem):
    return pltpu.CompilerParams(dimension_semantics=sem, vmem_limit_bytes=VMEM_LIMIT)


def _dot(a, b, prec=None):
    return jnp.dot(a, b, precision=prec, preferred_element_type=F32)


def _dot_nt(a, b, prec=None):
    return lax.dot_general(a, b, (((1,), (1,)), ((), ())), precision=prec, preferred_element_type=F32)


def _dot_bf16(a, b):
    return _dot(a.astype(BF16), b.astype(BF16))


def _dot_sel(sel, b, nt=False):
    f = _dot_nt if nt else _dot
    s = sel.astype(F32).astype(BF16)
    b1 = b.astype(BF16)
    r1 = b - b1.astype(F32)
    b2 = r1.astype(BF16)
    b3 = (r1 - b2.astype(F32)).astype(BF16)
    return f(s, b1) + f(s, b2) + f(s, b3)


def _rms(x, g):
    return x * lax.rsqrt(jnp.mean(x * x, axis=-1, keepdims=True) + EPS) * g


def _softplus(x):
    return jnp.maximum(x, 0.0) + jnp.log1p(jnp.exp(-jnp.abs(x)))


def _row_tile(n, pref):
    t = min(n, pref)
    while n % t:
        t //= 2
    return t


def _norm_matmul_kernel(x_ref, g_ref, w_ref, o_ref):
    y = _rms(x_ref[...], g_ref[...])
    o_ref[...] = _dot(y.astype(BF16), w_ref[...])


def norm_matmul(x, g, w):
    n, d = x.shape
    m = w.shape[1]
    tm = _row_tile(n, 256)
    return pl.pallas_call(
        _norm_matmul_kernel,
        grid=(n // tm,),
        in_specs=[
            pl.BlockSpec((tm, d), lambda i: (i, 0)),
            pl.BlockSpec((1, d), lambda i: (0, 0)),
            pl.BlockSpec((d, m), lambda i: (0, 0)),
        ],
        out_specs=pl.BlockSpec((tm, m), lambda i: (i, 0)),
        out_shape=jax.ShapeDtypeStruct((n, m), F32),
        compiler_params=_cparams(("parallel",)),
        name="norm_matmul",
    )(x, g.reshape(1, d), w)


def _matmul_res_kernel(*refs, n_in):
    ys = refs[:n_in]
    ws = refs[n_in:2 * n_in]
    res_ref = refs[2 * n_in]
    o_ref = refs[2 * n_in + 1]
    acc = res_ref[...]
    for y_ref, w_ref in zip(ys, ws):
        acc = acc + _dot(y_ref[...].astype(BF16), w_ref[...])
    o_ref[...] = acc


def matmul_res(ys, ws, res):
    n, d = res.shape
    tm = _row_tile(n, 512)
    n_in = len(ys)
    in_specs = [pl.BlockSpec((tm, y.shape[1]), lambda i: (i, 0)) for y in ys]
    in_specs += [pl.BlockSpec(w.shape, lambda i: (0, 0)) for w in ws]
    in_specs += [pl.BlockSpec((tm, d), lambda i: (i, 0))]
    return pl.pallas_call(
        functools.partial(_matmul_res_kernel, n_in=n_in),
        grid=(n // tm,),
        in_specs=in_specs,
        out_specs=pl.BlockSpec((tm, d), lambda i: (i, 0)),
        out_shape=jax.ShapeDtypeStruct((n, d), F32),
        compiler_params=_cparams(("parallel",)),
        name="matmul_res",
    )(*ys, *ws, res)


def _gdn_kernel(qkv_ref, z_ref, ba_ref, cprev_ref, s0_ref, cw_ref, prm_ref, an_ref,
                o_ref, sout_ref, xp_ref, s_ref, *, chunk):
    c = chunk
    nh = A_HEADS
    r = nh * c
    step = pl.program_id(1)

    @pl.when(step == 0)
    def _():
        xp_ref[5:8, :] = cprev_ref[0]
        for h in range(nh):
            s_ref[:, h * A_DK:(h + 1) * A_DK] = s0_ref[0, h]

    xp_ref[8:8 + c, :] = qkv_ref[0]
    conv = xp_ref[5:5 + c, :] * cw_ref[0:1, :]
    for j in range(1, CONV_W):
        conv = conv + xp_ref[5 + j:5 + j + c, :] * cw_ref[j:j + 1, :]
    xp_ref[5:8, :] = xp_ref[5 + c:8 + c, :]
    act = conv * jax.nn.sigmoid(conv)

    def stack(off):
        return jnp.concatenate([act[:, off + h * A_DK: off + (h + 1) * A_DK] for h in range(nh)], axis=0)

    qs, ks, vs = stack(0), stack(A_WIDTH), stack(2 * A_WIDTH)
    qs = qs * lax.rsqrt(jnp.sum(qs * qs, axis=-1, keepdims=True) + EPS) * (A_DK ** -0.5)
    ks = ks * lax.rsqrt(jnp.sum(ks * ks, axis=-1, keepdims=True) + EPS)

    ba = ba_ref[0]
    beta_full = jax.nn.sigmoid(ba)
    g_full = -jnp.exp(prm_ref[0:1, :]) * _softplus(ba + prm_ref[1:2, :])
    beta_s = jnp.concatenate([beta_full[:, h:h + 1] for h in range(nh)], axis=0)
    g_s = jnp.concatenate([g_full[:, nh + h:nh + h + 1] for h in range(nh)], axis=0)

    row = lax.broadcasted_iota(jnp.int32, (r, r), 0)
    col = lax.broadcasted_iota(jnp.int32, (r, r), 1)
    same = (row // c) == (col // c)
    m_incl = same & (col <= row)
    m_strict = same & (col < row)

    gcs = _dot_sel(m_incl, jnp.broadcast_to(g_s, (r, LANES)))
    lane0 = lax.broadcasted_iota(jnp.int32, (r, LANES), 1) == 0
    gcs_row = _dot_sel(lane0, gcs, nt=True)
    gc = gcs[:, 0:1]
    decay = jnp.exp(jnp.where(m_incl, gc - gcs_row, -jnp.inf))

    kb = ks * beta_s
    ks_b = ks.astype(BF16)
    a_mat = jnp.where(m_strict, _dot_nt(kb.astype(BF16), ks_b), 0.0) * decay
    eye = (row == col).astype(F32)
    p = -a_mat
    t_inv = eye + p
    for _ in range(int(math.log2(c)) - 1):
        p = _dot(p, p, HI)
        t_inv = t_inv + _dot(t_inv, p, HI)
    uw = _dot(t_inv, jnp.concatenate([vs * beta_s, kb * jnp.exp(gc)], axis=1), HI)
    u, w = uw[:, :A_DK], uw[:, A_DK:]
    qk = jnp.where(m_incl, _dot_nt(qs.astype(BF16), ks_b), 0.0) * decay

    def diag_blocks(x):
        return jnp.concatenate([x[h * c:(h + 1) * c, h * A_DK:(h + 1) * A_DK] for h in range(nh)], axis=0)

    s_cat = s_ref[...]
    wq_s = _dot_bf16(jnp.concatenate([w, qs * jnp.exp(gc)], axis=0), s_cat)
    v_new = u - diag_blocks(wq_s[:r])
    o_s = diag_blocks(wq_s[r:]) + _dot_bf16(qk, v_new)

    g_last = jnp.concatenate(
        [jnp.broadcast_to(gcs[(h + 1) * c - 1:(h + 1) * c, :], (c, LANES)) for h in range(nh)], axis=0)
    kdec = ks * jnp.exp(g_last - gcs)
    zero = jnp.zeros((c, A_DK), F32)
    v_exp = jnp.concatenate(
        [jnp.concatenate([v_new[h * c:(h + 1) * c] if hh == h else zero for hh in range(nh)], axis=1)
         for h in range(nh)], axis=0)
    s_scale = jnp.concatenate(
        [jnp.exp(jnp.broadcast_to(gcs[(h + 1) * c - 1:(h + 1) * c, :], (A_DK, LANES))) for h in range(nh)], axis=1)
    s_new = s_cat * s_scale + _dot_bf16(kdec.T, v_exp)
    s_ref[...] = s_new

    o_n = _rms(o_s, an_ref[...])
    o_full = jnp.concatenate([o_n[h * c:(h + 1) * c] for h in range(nh)], axis=1)
    zz = z_ref[0]
    o_ref[0] = o_full * (zz * jax.nn.sigmoid(zz))

    @pl.when(step == pl.num_programs(1) - 1)
    def _():
        for h in range(nh):
            sout_ref[0, h] = s_new[:, h * A_DK:(h + 1) * A_DK]


def gdn(proj, conv_prev, s0, conv_w, a_log, dt_bias, a_norm, chunk):
    b, t, _ = proj.shape
    nsteps = t // chunk
    prm = jnp.zeros((2, LANES), F32)
    prm = prm.at[0, A_HEADS:2 * A_HEADS].set(a_log).at[1, A_HEADS:2 * A_HEADS].set(dt_bias)
    qkv_w = 3 * A_WIDTH
    return pl.pallas_call(
        functools.partial(_gdn_kernel, chunk=chunk),
        grid=(b, nsteps),
        in_specs=[
            pl.BlockSpec((1, chunk, qkv_w), lambda i, j: (i, j, 0)),
            pl.BlockSpec((1, chunk, A_WIDTH), lambda i, j: (i, j, qkv_w // A_WIDTH)),
            pl.BlockSpec((1, chunk, LANES), lambda i, j: (i, j, (qkv_w + 3 * A_WIDTH) // LANES)),
            pl.BlockSpec((1, CONV_W - 1, qkv_w), lambda i, j: (i, 0, 0)),
            pl.BlockSpec((1, A_HEADS, A_DK, A_DK), lambda i, j: (i, 0, 0, 0)),
            pl.BlockSpec((CONV_W, qkv_w), lambda i, j: (0, 0)),
            pl.BlockSpec((2, LANES), lambda i, j: (0, 0)),
            pl.BlockSpec((1, A_DK), lambda i, j: (0, 0)),
        ],
        out_specs=[
            pl.BlockSpec((1, chunk, A_WIDTH), lambda i, j: (i, j, 0)),
            pl.BlockSpec((1, A_HEADS, A_DK, A_DK), lambda i, j: (i, 0, 0, 0)),
        ],
        out_shape=[
            jax.ShapeDtypeStruct((b, t, A_WIDTH), F32),
            jax.ShapeDtypeStruct((b, A_HEADS, A_DK, A_DK), F32),
        ],
        scratch_shapes=[
            pltpu.VMEM((chunk + 8, qkv_w), F32),
            pltpu.VMEM((A_DK, A_HEADS * A_DK), F32),
        ],
        compiler_params=_cparams(("parallel", "arbitrary")),
        name="gdn",
    )(proj, proj, proj, conv_prev, s0, conv_w, prm, a_norm.reshape(1, A_DK))


def _lru_kernel(xb_ref, gb_ref, cprev_ref, h0_ref, cw_ref, cb_ref, wa_ref, ba_ref, wx_ref, bx_ref, lam_ref,
                ob_ref, hl_ref, xp_ref, a_ref, u_ref, hs_ref, h_ref, *, tb):
    step = pl.program_id(1)

    @pl.when(step == 0)
    def _():
        xp_ref[5:8, :] = cprev_ref[0]
        h_ref[...] = h0_ref[0]

    xp_ref[8:8 + tb, :] = xb_ref[0]
    xc = xp_ref[5:5 + tb, :] * cw_ref[0:1, :]
    for j in range(1, CONV_W):
        xc = xc + xp_ref[5 + j:5 + j + tb, :] * cw_ref[j:j + 1, :]
    xp_ref[5:8, :] = xp_ref[5 + tb:8 + tb, :]
    xc = xc + cb_ref[...]
    xcb = xc.astype(BF16)
    rg = jax.nn.sigmoid(_dot(xcb, wa_ref[...]) + ba_ref[...])
    ig = jax.nn.sigmoid(_dot(xcb, wx_ref[...]) + bx_ref[...])
    log_a = -RG_C * rg * _softplus(-lam_ref[...])
    a_ref[...] = jnp.exp(log_a)
    u_ref[...] = jnp.sqrt(1.0 - jnp.exp(2.0 * log_a)) * (ig * xc)

    def body(i, h):
        h = a_ref[pl.ds(i, 1), :] * h + u_ref[pl.ds(i, 1), :]
        hs_ref[pl.ds(i, 1), :] = h
        return h

    h = lax.fori_loop(0, tb, body, h_ref[...], unroll=8)
    h_ref[...] = h
    ob_ref[0] = hs_ref[...] * jax.nn.gelu(gb_ref[0])

    @pl.when(step == pl.num_programs(1) - 1)
    def _():
        hl_ref[0] = h


def lru(proj, conv_prev, h0, conv_w, conv_b, wa_bd, ba, wx_bd, bx, lam):
    b, t, _ = proj.shape
    tb = _row_tile(t, 256)
    xb_blk = (3 * A_WIDTH + A_WIDTH) // B_WIDTH
    vec = lambda v: v.reshape(1, B_WIDTH)
    const = lambda shape: pl.BlockSpec(shape, lambda i, j: (0,) * len(shape))
    ob, hl = pl.pallas_call(
        functools.partial(_lru_kernel, tb=tb),
        grid=(b, t // tb),
        in_specs=[
            pl.BlockSpec((1, tb, B_WIDTH), lambda i, j: (i, j, xb_blk)),
            pl.BlockSpec((1, tb, B_WIDTH), lambda i, j: (i, j, xb_blk + 1)),
            pl.BlockSpec((1, CONV_W - 1, B_WIDTH), lambda i, j: (i, 0, 0)),
            pl.BlockSpec((1, 1, B_WIDTH), lambda i, j: (i, 0, 0)),
            const((CONV_W, B_WIDTH)), const((1, B_WIDTH)),
            const((B_WIDTH, B_WIDTH)), const((1, B_WIDTH)),
            const((B_WIDTH, B_WIDTH)), const((1, B_WIDTH)),
            const((1, B_WIDTH)),
        ],
        out_specs=[
            pl.BlockSpec((1, tb, B_WIDTH), lambda i, j: (i, j, 0)),
            pl.BlockSpec((1, 1, B_WIDTH), lambda i, j: (i, 0, 0)),
        ],
        out_shape=[
            jax.ShapeDtypeStruct((b, t, B_WIDTH), F32),
            jax.ShapeDtypeStruct((b, 1, B_WIDTH), F32),
        ],
        scratch_shapes=[
            pltpu.VMEM((tb + 8, B_WIDTH), F32),
            pltpu.VMEM((tb, B_WIDTH), F32),
            pltpu.VMEM((tb, B_WIDTH), F32),
            pltpu.VMEM((tb, B_WIDTH), F32),
            pltpu.VMEM((1, B_WIDTH), F32),
        ],
        compiler_params=_cparams(("parallel", "arbitrary")),
        name="lru",
    )(proj, proj, conv_prev, h0.reshape(b, 1, B_WIDTH), conv_w, vec(conv_b), wa_bd, vec(ba), wx_bd, vec(bx), vec(lam))
    return ob, hl.reshape(b, B_WIDTH)


def _attn_kernel(q_ref, k_ref, v_ref, lam_ref, ng_ref, o_ref, *, tq, nq, past, lam_init):
    qi = pl.program_id(2)
    q = q_ref[0] * (C_DH ** -0.5)
    lane = lax.broadcasted_iota(jnp.int32, q.shape, 1)
    qc = (jnp.where(lane < C_DH, q, 0.0).astype(BF16), jnp.where(lane >= C_DH, q, 0.0).astype(BF16))
    lv = lam_ref[...]
    lam = (jnp.exp(jnp.sum(lv[0:1] * lv[1:2], axis=-1, keepdims=True))
           - jnp.exp(jnp.sum(lv[2:3] * lv[3:4], axis=-1, keepdims=True)) + lam_init)
    rpos = lax.broadcasted_iota(jnp.int32, (tq, tq), 0)
    cpos = lax.broadcasted_iota(jnp.int32, (tq, tq), 1)
    tail_mask = (cpos // CHUNK) <= (rpos // CHUNK)

    def tile(i):
        p0 = past + i * tq
        kt = k_ref[0, p0:p0 + tq, :].astype(BF16)
        vt = v_ref[0, p0:p0 + tq, :].astype(BF16)
        if p0:
            kf = k_ref[0, 0:p0, :].astype(BF16)
            vf = v_ref[0, 0:p0, :].astype(BF16)
        o = None
        for comp in range(2):
            st = jnp.where(tail_mask, _dot_nt(qc[comp], kt), -jnp.inf)
            m = jnp.max(st, axis=-1, keepdims=True)
            if p0:
                sf = _dot_nt(qc[comp], kf)
                m = jnp.maximum(m, jnp.max(sf, axis=-1, keepdims=True))
            et = jnp.exp(st - m)
            l = jnp.sum(et, axis=-1, keepdims=True)
            acc = _dot(et.astype(BF16), vt)
            if p0:
                ef = jnp.exp(sf - m)
                l = l + jnp.sum(ef, axis=-1, keepdims=True)
                acc = acc + _dot(ef.astype(BF16), vf)
            oc = acc / l
            o = oc if comp == 0 else o - lam * oc
        o_ref[0] = _rms(o, ng_ref[...]) * (1.0 - lam_init)

    for i in range(nq):
        pl.when(qi == i)(functools.partial(tile, i))


def diff_attn(q_src, kv_src, q_col0, k_col0, v_col0, lam_vecs, norm_g, past, lam_init):
    b, t_q, _ = q_src.shape
    t_k = kv_src.shape[1]
    tq = _row_tile(t_q, 256)
    nq = t_q // tq
    assert t_k == past + t_q and past % CHUNK == 0 and past % 8 == 0
    assert nq == 1 or tq % CHUNK == 0
    return pl.pallas_call(
        functools.partial(_attn_kernel, tq=tq, nq=nq, past=past, lam_init=lam_init),
        grid=(b, C_HEADS, t_q // tq),
        in_specs=[
            pl.BlockSpec((1, tq, C_VD), lambda i, h, j: (i, j, q_col0 + h)),
            pl.BlockSpec((1, t_k, C_VD), lambda i, h, j: (i, 0, k_col0 + h)),
            pl.BlockSpec((1, t_k, C_VD), lambda i, h, j: (i, 0, v_col0 + h)),
            pl.BlockSpec((4, C_DH), lambda i, h, j: (0, 0)),
            pl.BlockSpec((1, C_VD), lambda i, h, j: (0, 0)),
        ],
        out_specs=pl.BlockSpec((1, tq, C_VD), lambda i, h, j: (i, j, h)),
        out_shape=jax.ShapeDtypeStruct((b, t_q, C_WIDTH), F32),
        compiler_params=_cparams(("parallel", "parallel", "arbitrary")),
        name="diff_attn",
    )(q_src, kv_src, kv_src, lam_vecs, norm_g.reshape(1, C_VD))


def _topk_rows(vals, k, payload=None):
    n_rows = vals.shape[0]
    rio = lax.broadcasted_iota(jnp.int32, vals.shape, 0)
    out_v, out_p = [], []
    for _ in range(k):
        m = jnp.max(vals, axis=0, keepdims=True)
        am = jnp.min(jnp.where(vals == m, rio, n_rows), axis=0, keepdims=True)
        hit = rio == am
        out_v.append(m)
        out_p.append(am if payload is None else jnp.max(jnp.where(hit, payload, -1), axis=0, keepdims=True))
        vals = jnp.where(hit, -jnp.inf, vals)
    return jnp.concatenate(out_v, axis=0), jnp.concatenate(out_p, axis=0)


_CAND_PER_ROW = [PEER_TOPK // (i + 1) for i in range(PEER_TOPK)]
_N_CAND = sum(_CAND_PER_ROW)
_N_CAND_PAD = -(-_N_CAND // 8) * 8


def _peer_topk_kernel(q_ref, keys_ref, idx_ref, gate_ref, idx_t, gate_t, cs_ref, ce_ref):
    tt = q_ref.shape[0]
    half = PEER_DQ // 2
    cs_ref[_N_CAND:, :] = jnp.full((_N_CAND_PAD - _N_CAND, tt), -jnp.inf, F32)
    ce_ref[_N_CAND:, :] = jnp.full((_N_CAND_PAD - _N_CAND, tt), -1, jnp.int32)

    def head(h, carry):
        tops = []
        for p in range(2):
            off = pl.multiple_of(h * PEER_DQ + p * half, half)
            q = q_ref[:, pl.ds(off, half)]
            sc = _dot_nt(keys_ref[p], q, HI)
            tops.append(_topk_rows(sc, PEER_TOPK))
        (s1, i1), (s2, i2) = tops
        row0 = 0
        for i, cnt in enumerate(_CAND_PER_ROW):
            cs_ref[row0:row0 + cnt, :] = s1[i:i + 1] + s2[0:cnt]
            ce_ref[row0:row0 + cnt, :] = i1[i:i + 1] * N_KEYS + i2[0:cnt]
            row0 += cnt
        top_s, top_e = _topk_rows(cs_ref[...], PEER_TOPK, ce_ref[...])
        e = jnp.exp(top_s - top_s[0:1])
        gate = e / jnp.sum(e, axis=0, keepdims=True)
        r0 = pl.multiple_of(h * PEER_TOPK, PEER_TOPK)
        idx_t[pl.ds(r0, PEER_TOPK), :] = top_e
        gate_t[pl.ds(r0, PEER_TOPK), :] = gate
        return carry

    lax.fori_loop(0, PEER_HEADS, head, 0)
    idx_ref[...] = idx_t[...].T
    gate_ref[...] = gate_t[...].T


def peer_topk(qry, sub_keys):
    n = qry.shape[0]
    tt = _row_tile(n, 256)
    nsel = PEER_HEADS * PEER_TOPK
    return pl.pallas_call(
        _peer_topk_kernel,
        grid=(n // tt,),
        in_specs=[
            pl.BlockSpec((tt, PEER_HEADS * PEER_DQ), lambda i: (i, 0)),
            pl.BlockSpec((2, N_KEYS, PEER_DQ // 2), lambda i: (0, 0, 0)),
        ],
        out_specs=[
            pl.BlockSpec((tt, nsel), lambda i: (i, 0)),
            pl.BlockSpec((tt, nsel), lambda i: (i, 0)),
        ],
        out_shape=[
            jax.ShapeDtypeStruct((n, nsel), jnp.int32),
            jax.ShapeDtypeStruct((n, nsel), F32),
        ],
        scratch_shapes=[pltpu.VMEM((nsel, tt), jnp.int32), pltpu.VMEM((nsel, tt), F32),
                        pltpu.VMEM((_N_CAND_PAD, tt), F32), pltpu.VMEM((_N_CAND_PAD, tt), jnp.int32)],
        compiler_params=_cparams(("parallel",)),
        name="peer_topk",
    )(qry, sub_keys)


PEER_GROUP = 8
NSEL = PEER_HEADS * PEER_TOPK
SLAB_ROWS = 16
SLAB_PITCH = 20


def peer_table(u, v):
    ne, d = u.shape
    assert 2 * d == SLAB_ROWS * LANES
    return jnp.concatenate([u.reshape(ne, d // LANES, LANES), v.reshape(ne, d // LANES, LANES)], axis=1)


def _peer_expert_kernel(idx_ref, idxn_ref, x_ref, gate_ref, g_ref, gf_ref, uv_hbm, o_ref, gbuf0, gbuf1, sem,
                        *, final_norm):
    g = PEER_GROUP
    d = x_ref.shape[1]
    nsub = d // LANES
    step = pl.program_id(0)
    gbufs = (gbuf0, gbuf1)

    def slab_copy(e, slot, t, r):
        row0 = (t * NSEL + r) * SLAB_PITCH
        return pltpu.make_async_copy(uv_hbm.at[e], gbufs[slot].at[pl.ds(row0, SLAB_ROWS), :], sem.at[slot])

    def issue_token(iref, t_src, slot, t):
        for r in range(NSEL):
            slab_copy(iref[t_src, r], slot, t, r).start(priority=r % 2)

    def wait_group(slot):
        def body(t, c):
            for r in range(NSEL):
                slab_copy(0, slot, 0, 0).wait()
            return c
        lax.fori_loop(0, g, body, 0)

    def expert_rows(slot, t, first):
        base = t * NSEL * SLAB_PITCH + first
        return jnp.concatenate(
            [gbufs[slot][pl.ds(base + s, NSEL, stride=SLAB_PITCH), :] for s in range(nsub)], axis=1)

    def combine_group(row0, slot, iref, t_src0, slot_next):
        x = x_ref[row0:row0 + g, :]
        xnb = _rms(x, g_ref[...]).astype(BF16)
        gate = gate_ref[row0:row0 + g, :]
        rio = lax.broadcasted_iota(jnp.int32, (g, NSEL), 0)
        out = x
        for t in range(g):
            issue_token(iref, t_src0 + t, slot_next, t)
            act = _dot_nt(xnb, expert_rows(slot, t, 0).astype(BF16))
            w = jnp.where(rio == t, gate * jax.nn.gelu(act), 0.0).astype(BF16)
            out = out + _dot(w, expert_rows(slot, t, SLAB_ROWS // 2).astype(BF16))
        if final_norm:
            out = _rms(out, gf_ref[...])
        o_ref[row0:row0 + g, :] = out

    @pl.when(step == 0)
    def _():
        for t in range(g):
            issue_token(idx_ref, t, 0, t)

    wait_group(0)
    combine_group(0, 0, idx_ref, g, 1)
    wait_group(1)
    combine_group(g, 1, idxn_ref, 0, 0)

    @pl.when(step == pl.num_programs(0) - 1)
    def _():
        wait_group(0)


def peer_experts(x, idx, gate, norm_g, final_g, uv, final_norm):
    n, d = x.shape
    g = PEER_GROUP
    nsteps = n // (2 * g)
    ngroups = 2 * nsteps
    return pl.pallas_call(
        functools.partial(_peer_expert_kernel, final_norm=final_norm),
        grid=(nsteps,),
        in_specs=[
            pl.BlockSpec((2 * g, NSEL), lambda i: (i, 0), memory_space=pltpu.SMEM),
            pl.BlockSpec((g, NSEL), lambda i: (jnp.minimum(2 * i + 2, ngroups - 1), 0), memory_space=pltpu.SMEM),
            pl.BlockSpec((2 * g, d), lambda i: (i, 0)),
            pl.BlockSpec((2 * g, NSEL), lambda i: (i, 0)),
            pl.BlockSpec((1, d), lambda i: (0, 0)),
            pl.BlockSpec((1, d), lambda i: (0, 0)),
            pl.BlockSpec(memory_space=pl.ANY),
        ],
        out_specs=pl.BlockSpec((2 * g, d), lambda i: (i, 0)),
        out_shape=jax.ShapeDtypeStruct((n, d), F32),
        scratch_shapes=[
            pltpu.VMEM((g * NSEL * SLAB_PITCH, LANES), F32),
            pltpu.VMEM((g * NSEL * SLAB_PITCH, LANES), F32),
            pltpu.SemaphoreType.DMA((2,)),
        ],
        compiler_params=_cparams(("arbitrary",)),
        name="peer_experts",
    )(idx, idx, x, gate, norm_g.reshape(1, d), final_g.reshape(1, d), uv)


def _block_diag(w):
    nb, bi, bj = w.shape
    out = jnp.zeros((nb * bi, nb * bj), w.dtype)
    for i in range(nb):
        out = out.at[i * bi:(i + 1) * bi, i * bj:(i + 1) * bj].set(w[i])
    return out


def _prep_weights(p):
    depth = p['norm_mix'].shape[0]
    d = p['norm_final'].shape[0]
    w = {}
    a3, a4 = 3 * A_WIDTH, 4 * A_WIDTH
    w['ab_in'], w['ab_out_a'], w['ab_out_b'], w['ga'], w['gx'] = [], [], [], [], []
    for j in range(p['ab_w_in'].shape[0]):
        wi = p['ab_w_in'][j]
        pad = jnp.zeros((d, LANES - 2 * A_HEADS), wi.dtype)
        cols = [wi[:, :a4], wi[:, a4 + 2 * A_HEADS:], wi[:, a4:a4 + 2 * A_HEADS], pad]
        w['ab_in'].append(jnp.concatenate(cols, axis=1).astype(BF16))
        w['ab_out_a'].append(p['ab_w_out'][j][:A_WIDTH].astype(BF16))
        w['ab_out_b'].append(p['ab_w_out'][j][A_WIDTH:].astype(BF16))
        w['ga'].append(_block_diag(p['b_gate_a_w'][j]).astype(BF16))
        w['gx'].append(_block_diag(p['b_gate_x_w'][j]).astype(BF16))
    w['c_qkv'] = [m.astype(BF16) for m in p['c_w_qkv']]
    w['c_out'] = [m.astype(BF16) for m in p['c_w_out']]
    w['c_lam'] = [jnp.stack([p['c_lambda_q1'][j], p['c_lambda_k1'][j], p['c_lambda_q2'][j], p['c_lambda_k2'][j]])
                  for j in range(p['c_w_qkv'].shape[0])]
    w['peer_q'] = [p['peer_w_query'][l].astype(BF16) for l in range(depth)]
    w['peer_uv'] = [peer_table(p['peer_u'][l], p['peer_v'][l]) for l in range(depth)]
    return w


def _peer(x2, l, p, w, final_norm):
    qry = norm_matmul(x2, p['norm_ffn'][l], w['peer_q'][l])
    idx, gate = peer_topk(qry, p['peer_sub_keys'][l])
    return peer_experts(x2, idx, gate, p['norm_ffn'][l], p['norm_final'], w['peer_uv'][l], final_norm)


def _trunk(x, st_a_conv, st_a_delta, st_b_conv, st_b_h, c_k, c_v, p, w, past):
    b, t, d = x.shape
    n = b * t
    chunk = min(CHUNK, t)
    depth = p['norm_mix'].shape[0]
    x2 = x.reshape(n, d)
    a_conv, a_delta, b_conv, b_h, new_k, new_v = [], [], [], [], [], []
    for l in range(depth):
        j = l // 2
        if l % 2 == 0:
            proj = norm_matmul(x2, p['norm_mix'][l], w['ab_in'][j]).reshape(b, t, -1)
            o_a, s_new = gdn(proj, st_a_conv[j], st_a_delta[j], p['a_conv_w'][j], p['a_log'][j],
                             p['a_dt_bias'][j], p['a_norm'][j], chunk)
            o_b, h_last = lru(proj, st_b_conv[j], st_b_h[j], p['b_conv_w'][j], p['b_conv_b'][j], w['ga'][j],
                              p['b_gate_a_b'][j], w['gx'][j], p['b_gate_x_b'][j], p['b_lambda'][j])
            a3 = 3 * A_WIDTH
            xb0 = a3 + A_WIDTH
            a_conv.append(jnp.concatenate([st_a_conv[j], proj[:, :, :a3]], axis=1)[:, -(CONV_W - 1):])
            b_conv.append(jnp.concatenate([st_b_conv[j], proj[:, :, xb0:xb0 + B_WIDTH]], axis=1)[:, -(CONV_W - 1):])
            a_delta.append(s_new)
            b_h.append(h_last)
            x2 = matmul_res([o_a.reshape(n, A_WIDTH), o_b.reshape(n, B_WIDTH)],
                            [w['ab_out_a'][j], w['ab_out_b'][j]], x2)
        else:
            lam_init = 0.8 - 0.6 * math.exp(-0.3 * l)
            qkv = norm_matmul(x2, p['norm_mix'][l], w['c_qkv'][j]).reshape(b, t, 3 * C_WIDTH)
            kn = qkv[:, :, C_WIDTH:2 * C_WIDTH]
            vn = qkv[:, :, 2 * C_WIDTH:]
            if c_k is None:
                o_c = diff_attn(qkv, qkv, 0, C_HEADS, 2 * C_HEADS, w['c_lam'][j], p['c_norm'][j], past, lam_init)
            else:
                k_all = jnp.concatenate([c_k[j].reshape(b, -1, C_WIDTH), kn], axis=1)
                v_all = jnp.concatenate([c_v[j].reshape(b, -1, C_WIDTH), vn], axis=1)
                kv = jnp.concatenate([k_all, v_all], axis=2)
                o_c = diff_attn(qkv, kv, 0, 0, C_HEADS, w['c_lam'][j], p['c_norm'][j], past, lam_init)
            new_k.append(kn.reshape(b, t, C_HEADS, 2, C_DH))
            new_v.append(vn.reshape(b, t, C_HEADS, C_VD))
            x2 = matmul_res([o_c.reshape(n, C_WIDTH)], [w['c_out'][j]], x2)
        x2 = _peer(x2, l, p, w, final_norm=(l == depth - 1))
    return (x2.reshape(b, t, d), jnp.stack(a_conv), jnp.stack(a_delta), jnp.stack(b_conv), jnp.stack(b_h),
            jnp.stack(new_k), jnp.stack(new_v))


def kernel(x_prompt, x_sample, state_a_conv, state_a_delta, state_b_conv, state_b_rglru, cache_c_k, cache_c_v, norm_mix, norm_ffn, norm_final, ab_w_in, ab_w_out, a_conv_w, a_log, a_dt_bias, a_norm, b_conv_w, b_conv_b, b_gate_a_w, b_gate_a_b, b_gate_x_w, b_gate_x_b, b_lambda, c_w_qkv, c_w_out, c_lambda_q1, c_lambda_k1, c_lambda_q2, c_lambda_k2, c_norm, peer_w_query, peer_sub_keys, peer_u, peer_v):
    p = dict(norm_mix=norm_mix, norm_ffn=norm_ffn, norm_final=norm_final, ab_w_in=ab_w_in,
             ab_w_out=ab_w_out, a_conv_w=a_conv_w, a_log=a_log, a_dt_bias=a_dt_bias, a_norm=a_norm,
             b_conv_w=b_conv_w, b_conv_b=b_conv_b, b_gate_a_w=b_gate_a_w, b_gate_a_b=b_gate_a_b,
             b_gate_x_w=b_gate_x_w, b_gate_x_b=b_gate_x_b, b_lambda=b_lambda, c_w_qkv=c_w_qkv,
             c_w_out=c_w_out, c_lambda_q1=c_lambda_q1, c_lambda_k1=c_lambda_k1,
             c_lambda_q2=c_lambda_q2, c_lambda_k2=c_lambda_k2, c_norm=c_norm,
             peer_w_query=peer_w_query, peer_sub_keys=peer_sub_keys, peer_u=peer_u, peer_v=peer_v)
    w = _prep_weights(p)
    bp = x_prompt.shape[0]
    dt = x_prompt.dtype
    n_even = state_a_conv.shape[0]
    z_a_conv = jnp.zeros((n_even, bp) + state_a_conv.shape[2:], dt)
    z_a_delta = jnp.zeros((n_even, bp) + state_a_delta.shape[2:], dt)
    z_b_conv = jnp.zeros((n_even, bp) + state_b_conv.shape[2:], dt)
    z_b_h = jnp.zeros((n_even, bp) + state_b_rglru.shape[2:], dt)
    (y_prompt, pa_conv, pa_delta, pb_conv, pb_h, pc_k, pc_v) = _trunk(
        x_prompt, z_a_conv, z_a_delta, z_b_conv, z_b_h, None, None, p, w, 0)
    (y_sample, sa_conv, sa_delta, sb_conv, sb_h, sc_k, sc_v) = _trunk(
        x_sample, state_a_conv, state_a_delta, state_b_conv, state_b_rglru, cache_c_k, cache_c_v,
        p, w, cache_c_k.shape[2])
    return (y_prompt, y_sample, pa_conv, pa_delta, pb_conv, pb_h, pc_k, pc_v,
            sa_conv, sa_delta, sb_conv, sb_h, sc_k, sc_v)
```

```python
import functools
import math

import jax
import jax.numpy as jnp
from jax import lax
from jax.experimental import pallas as pl
from jax.experimental.pallas import tpu as pltpu

F32 = jnp.float32
BF16 = jnp.bfloat16
HI = lax.Precision.HIGHEST

EPS = 1e-6
CHUNK = 64
CONV_W = 4
A_HEADS = 4
A_DK = 128
A_WIDTH = A_HEADS * A_DK
B_WIDTH = 512
B_BLOCKS = 8
RG_C = 8.0
C_HEADS = 8
C_DH = 64
C_VD = 2 * C_DH
C_WIDTH = C_HEADS * C_VD
PEER_HEADS = 8
N_KEYS = 128
PEER_TOPK = 16
PEER_DQ = 256
LANES = 128
VMEM_LIMIT = 48 * 1024 * 1024


def _cparams(sem):
    return pltpu.CompilerParams(dimension_semantics=sem, vmem_limit_bytes=VMEM_LIMIT)


def _dot(a, b, prec=None):
    return jnp.dot(a, b, precision=prec, preferred_element_type=F32)


def _dot_nt(a, b, prec=None):
    return lax.dot_general(a, b, (((1,), (1,)), ((), ())), precision=prec, preferred_element_type=F32)


def _dot_bf16(a, b):
    return _dot(a.astype(BF16), b.astype(BF16))


def _dot_sel(sel, b, nt=False):
    f = _dot_nt if nt else _dot
    s = sel.astype(F32).astype(BF16)
    b1 = b.astype(BF16)
    r1 = b - b1.astype(F32)
    b2 = r1.astype(BF16)
    b3 = (r1 - b2.astype(F32)).astype(BF16)
    return f(s, b1) + f(s, b2) + f(s, b3)


def _rms(x, g):
    return x * lax.rsqrt(jnp.mean(x * x, axis=-1, keepdims=True) + EPS) * g


def _softplus(x):
    return jnp.maximum(x, 0.0) + jnp.log1p(jnp.exp(-jnp.abs(x)))


def _row_tile(n, pref):
    t = min(n, pref)
    while n % t:
        t //= 2
    return t


def _norm_matmul_kernel(x_ref, g_ref, w_ref, o_ref):
    y = _rms(x_ref[...], g_ref[...])
    o_ref[...] = _dot(y.astype(BF16), w_ref[...])


def norm_matmul(x, g, w):
    n, d = x.shape
    m = w.shape[1]
    tm = _row_tile(n, 256)
    return pl.pallas_call(
        _norm_matmul_kernel,
        grid=(n // tm,),
        in_specs=[
            pl.BlockSpec((tm, d), lambda i: (i, 0)),
            pl.BlockSpec((1, d), lambda i: (0, 0)),
            pl.BlockSpec((d, m), lambda i: (0, 0)),
        ],
        out_specs=pl.BlockSpec((tm, m), lambda i: (i, 0)),
        out_shape=jax.ShapeDtypeStruct((n, m), F32),
        compiler_params=_cparams(("parallel",)),
        name="norm_matmul",
    )(x, g.reshape(1, d), w)


def _matmul_res_kernel(*refs, n_in):
    ys = refs[:n_in]
    ws = refs[n_in:2 * n_in]
    res_ref = refs[2 * n_in]
    o_ref = refs[2 * n_in + 1]
    acc = res_ref[...]
    for y_ref, w_ref in zip(ys, ws):
        acc = acc + _dot(y_ref[...].astype(BF16), w_ref[...])
    o_ref[...] = acc


def matmul_res(ys, ws, res):
    n, d = res.shape
    tm = _row_tile(n, 512)
    n_in = len(ys)
    in_specs = [pl.BlockSpec((tm, y.shape[1]), lambda i: (i, 0)) for y in ys]
    in_specs += [pl.BlockSpec(w.shape, lambda i: (0, 0)) for w in ws]
    in_specs += [pl.BlockSpec((tm, d), lambda i: (i, 0))]
    return pl.pallas_call(
        functools.partial(_matmul_res_kernel, n_in=n_in),
        grid=(n // tm,),
        in_specs=in_specs,
        out_specs=pl.BlockSpec((tm, d), lambda i: (i, 0)),
        out_shape=jax.ShapeDtypeStruct((n, d), F32),
        compiler_params=_cparams(("parallel",)),
        name="matmul_res",
    )(*ys, *ws, res)


def _gdn_kernel(qkv_ref, z_ref, ba_ref, cprev_ref, s0_ref, cw_ref, prm_ref, an_ref,
                o_ref, sout_ref, xp_ref, s_ref, *, chunk):
    c = chunk
    nh = A_HEADS
    r = nh * c
    step = pl.program_id(1)

    @pl.when(step == 0)
    def _():
        xp_ref[5:8, :] = cprev_ref[0]
        for h in range(nh):
            s_ref[:, h * A_DK:(h + 1) * A_DK] = s0_ref[0, h]

    xp_ref[8:8 + c, :] = qkv_ref[0]
    conv = xp_ref[5:5 + c, :] * cw_ref[0:1, :]
    for j in range(1, CONV_W):
        conv = conv + xp_ref[5 + j:5 + j + c, :] * cw_ref[j:j + 1, :]
    xp_ref[5:8, :] = xp_ref[5 + c:8 + c, :]
    act = conv * jax.nn.sigmoid(conv)

    def stack(off):
        return jnp.concatenate([act[:, off + h * A_DK: off + (h + 1) * A_DK] for h in range(nh)], axis=0)

    qs, ks, vs = stack(0), stack(A_WIDTH), stack(2 * A_WIDTH)
    qs = qs * lax.rsqrt(jnp.sum(qs * qs, axis=-1, keepdims=True) + EPS) * (A_DK ** -0.5)
    ks = ks * lax.rsqrt(jnp.sum(ks * ks, axis=-1, keepdims=True) + EPS)

    ba = ba_ref[0]
    beta_full = jax.nn.sigmoid(ba)
    g_full = -jnp.exp(prm_ref[0:1, :]) * _softplus(ba + prm_ref[1:2, :])
    beta_s = jnp.concatenate([beta_full[:, h:h + 1] for h in range(nh)], axis=0)
    g_s = jnp.concatenate([g_full[:, nh + h:nh + h + 1] for h in range(nh)], axis=0)

    row = lax.broadcasted_iota(jnp.int32, (r, r), 0)
    col = lax.broadcasted_iota(jnp.int32, (r, r), 1)
    same = (row // c) == (col // c)
    m_incl = same & (col <= row)
    m_strict = same & (col < row)

    gcs = _dot_sel(m_incl, jnp.broadcast_to(g_s, (r, LANES)))
    lane0 = lax.broadcasted_iota(jnp.int32, (r, LANES), 1) == 0
    gcs_row = _dot_sel(lane0, gcs, nt=True)
    gc = gcs[:, 0:1]
    decay = jnp.exp(jnp.where(m_incl, gc - gcs_row, -jnp.inf))

    kb = ks * beta_s
    ks_b = ks.astype(BF16)
    a_mat = jnp.where(m_strict, _dot_nt(kb.astype(BF16), ks_b), 0.0) * decay
    eye = (row == col).astype(F32)
    p = -a_mat
    t_inv = eye + p
    for _ in range(int(math.log2(c)) - 1):
        p = _dot(p, p, HI)
        t_inv = t_inv + _dot(t_inv, p, HI)
    uw = _dot(t_inv, jnp.concatenate([vs * beta_s, kb * jnp.exp(gc)], axis=1), HI)
    u, w = uw[:, :A_DK], uw[:, A_DK:]
    qk = jnp.where(m_incl, _dot_nt(qs.astype(BF16), ks_b), 0.0) * decay

    def diag_blocks(x):
        return jnp.concatenate([x[h * c:(h + 1) * c, h * A_DK:(h + 1) * A_DK] for h in range(nh)], axis=0)

    s_cat = s_ref[...]
    wq_s = _dot_bf16(jnp.concatenate([w, qs * jnp.exp(gc)], axis=0), s_cat)
    v_new = u - diag_blocks(wq_s[:r])
    o_s = diag_blocks(wq_s[r:]) + _dot_bf16(qk, v_new)

    g_last = jnp.concatenate(
        [jnp.broadcast_to(gcs[(h + 1) * c - 1:(h + 1) * c, :], (c, LANES)) for h in range(nh)], axis=0)
    kdec = ks * jnp.exp(g_last - gcs)
    zero = jnp.zeros((c, A_DK), F32)
    v_exp = jnp.concatenate(
        [jnp.concatenate([v_new[h * c:(h + 1) * c] if hh == h else zero for hh in range(nh)], axis=1)
         for h in range(nh)], axis=0)
    s_scale = jnp.concatenate(
        [jnp.exp(jnp.broadcast_to(gcs[(h + 1) * c - 1:(h + 1) * c, :], (A_DK, LANES))) for h in range(nh)], axis=1)
    s_new = s_cat * s_scale + _dot_bf16(kdec.T, v_exp)
    s_ref[...] = s_new

    o_n = _rms(o_s, an_ref[...])
    o_full = jnp.concatenate([o_n[h * c:(h + 1) * c] for h in range(nh)], axis=1)
    zz = z_ref[0]
    o_ref[0] = o_full * (zz * jax.nn.sigmoid(zz))

    @pl.when(step == pl.num_programs(1) - 1)
    def _():
        for h in range(nh):
            sout_ref[0, h] = s_new[:, h * A_DK:(h + 1) * A_DK]


def gdn(proj, conv_prev, s0, conv_w, a_log, dt_bias, a_norm, chunk):
    b, t, _ = proj.shape
    nsteps = t // chunk
    prm = jnp.zeros((2, LANES), F32)
    prm = prm.at[0, A_HEADS:2 * A_HEADS].set(a_log).at[1, A_HEADS:2 * A_HEADS].set(dt_bias)
    qkv_w = 3 * A_WIDTH
    return pl.pallas_call(
        functools.partial(_gdn_kernel, chunk=chunk),
        grid=(b, nsteps),
        in_specs=[
            pl.BlockSpec((1, chunk, qkv_w), lambda i, j: (i, j, 0)),
            pl.BlockSpec((1, chunk, A_WIDTH), lambda i, j: (i, j, qkv_w // A_WIDTH)),
            pl.BlockSpec((1, chunk, LANES), lambda i, j: (i, j, (qkv_w + 3 * A_WIDTH) // LANES)),
            pl.BlockSpec((1, CONV_W - 1, qkv_w), lambda i, j: (i, 0, 0)),
            pl.BlockSpec((1, A_HEADS, A_DK, A_DK), lambda i, j: (i, 0, 0, 0)),
            pl.BlockSpec((CONV_W, qkv_w), lambda i, j: (0, 0)),
            pl.BlockSpec((2, LANES), lambda i, j: (0, 0)),
            pl.BlockSpec((1, A_DK), lambda i, j: (0, 0)),
        ],
        out_specs=[
            pl.BlockSpec((1, chunk, A_WIDTH), lambda i, j: (i, j, 0)),
            pl.BlockSpec((1, A_HEADS, A_DK, A_DK), lambda i, j: (i, 0, 0, 0)),
        ],
        out_shape=[
            jax.ShapeDtypeStruct((b, t, A_WIDTH), F32),
            jax.ShapeDtypeStruct((b, A_HEADS, A_DK, A_DK), F32),
        ],
        scratch_shapes=[
            pltpu.VMEM((chunk + 8, qkv_w), F32),
            pltpu.VMEM((A_DK, A_HEADS * A_DK), F32),
        ],
        compiler_params=_cparams(("parallel", "arbitrary")),
        name="gdn",
    )(proj, proj, proj, conv_prev, s0, conv_w, prm, a_norm.reshape(1, A_DK))


def _lru_kernel(xb_ref, gb_ref, cprev_ref, h0_ref, cw_ref, cb_ref, wa_ref, ba_ref, wx_ref, bx_ref, lam_ref,
                ob_ref, hl_ref, xp_ref, a_ref, u_ref, hs_ref, h_ref, *, tb):
    step = pl.program_id(1)

    @pl.when(step == 0)
    def _():
        xp_ref[5:8, :] = cprev_ref[0]
        h_ref[...] = h0_ref[0]

    xp_ref[8:8 + tb, :] = xb_ref[0]
    xc = xp_ref[5:5 + tb, :] * cw_ref[0:1, :]
    for j in range(1, CONV_W):
        xc = xc + xp_ref[5 + j:5 + j + tb, :] * cw_ref[j:j + 1, :]
    xp_ref[5:8, :] = xp_ref[5 + tb:8 + tb, :]
    xc = xc + cb_ref[...]
    xcb = xc.astype(BF16)
    rg = jax.nn.sigmoid(_dot(xcb, wa_ref[...]) + ba_ref[...])
    ig = jax.nn.sigmoid(_dot(xcb, wx_ref[...]) + bx_ref[...])
    log_a = -RG_C * rg * _softplus(-lam_ref[...])
    a_ref[...] = jnp.exp(log_a)
    u_ref[...] = jnp.sqrt(1.0 - jnp.exp(2.0 * log_a)) * (ig * xc)

    def body(i, h):
        h = a_ref[pl.ds(i, 1), :] * h + u_ref[pl.ds(i, 1), :]
        hs_ref[pl.ds(i, 1), :] = h
        return h

    h = lax.fori_loop(0, tb, body, h_ref[...], unroll=8)
    h_ref[...] = h
    ob_ref[0] = hs_ref[...] * jax.nn.gelu(gb_ref[0])

    @pl.when(step == pl.num_programs(1) - 1)
    def _():
        hl_ref[0] = h


def lru(proj, conv_prev, h0, conv_w, conv_b, wa_bd, ba, wx_bd, bx, lam):
    b, t, _ = proj.shape
    tb = _row_tile(t, 256)
    xb_blk = (3 * A_WIDTH + A_WIDTH) // B_WIDTH
    vec = lambda v: v.reshape(1, B_WIDTH)
    const = lambda shape: pl.BlockSpec(shape, lambda i, j: (0,) * len(shape))
    ob, hl = pl.pallas_call(
        functools.partial(_lru_kernel, tb=tb),
        grid=(b, t // tb),
        in_specs=[
            pl.BlockSpec((1, tb, B_WIDTH), lambda i, j: (i, j, xb_blk)),
            pl.BlockSpec((1, tb, B_WIDTH), lambda i, j: (i, j, xb_blk + 1)),
            pl.BlockSpec((1, CONV_W - 1, B_WIDTH), lambda i, j: (i, 0, 0)),
            pl.BlockSpec((1, 1, B_WIDTH), lambda i, j: (i, 0, 0)),
            const((CONV_W, B_WIDTH)), const((1, B_WIDTH)),
            const((B_WIDTH, B_WIDTH)), const((1, B_WIDTH)),
            const((B_WIDTH, B_WIDTH)), const((1, B_WIDTH)),
            const((1, B_WIDTH)),
        ],
        out_specs=[
            pl.BlockSpec((1, tb, B_WIDTH), lambda i, j: (i, j, 0)),
            pl.BlockSpec((1, 1, B_WIDTH), lambda i, j: (i, 0, 0)),
        ],
        out_shape=[
            jax.ShapeDtypeStruct((b, t, B_WIDTH), F32),
            jax.ShapeDtypeStruct((b, 1, B_WIDTH), F32),
        ],
        scratch_shapes=[
            pltpu.VMEM((tb + 8, B_WIDTH), F32),
            pltpu.VMEM((tb, B_WIDTH), F32),
            pltpu.VMEM((tb, B_WIDTH), F32),
            pltpu.VMEM((tb, B_WIDTH), F32),
            pltpu.VMEM((1, B_WIDTH), F32),
        ],
        compiler_params=_cparams(("parallel", "arbitrary")),
        name="lru",
    )(proj, proj, conv_prev, h0.reshape(b, 1, B_WIDTH), conv_w, vec(conv_b), wa_bd, vec(ba), wx_bd, vec(bx), vec(lam))
    return ob, hl.reshape(b, B_WIDTH)


def _attn_kernel(q_ref, k_ref, v_ref, lam_ref, ng_ref, o_ref, *, tq, nq, past, lam_init):
    qi = pl.program_id(2)
    q = q_ref[0] * (C_DH ** -0.5)
    lane = lax.broadcasted_iota(jnp.int32, q.shape, 1)
    qc = (jnp.where(lane < C_DH, q, 0.0).astype(BF16), jnp.where(lane >= C_DH, q, 0.0).astype(BF16))
    lv = lam_ref[...]
    lam = (jnp.exp(jnp.sum(lv[0:1] * lv[1:2], axis=-1, keepdims=True))
           - jnp.exp(jnp.sum(lv[2:3] * lv[3:4], axis=-1, keepdims=True)) + lam_init)
    rpos = lax.broadcasted_iota(jnp.int32, (tq, tq), 0)
    cpos = lax.broadcasted_iota(jnp.int32, (tq, tq), 1)
    tail_mask = (cpos // CHUNK) <= (rpos // CHUNK)

    def tile(i):
        p0 = past + i * tq
        kt = k_ref[0, p0:p0 + tq, :].astype(BF16)
        vt = v_ref[0, p0:p0 + tq, :].astype(BF16)
        if p0:
            kf = k_ref[0, 0:p0, :].astype(BF16)
            vf = v_ref[0, 0:p0, :].astype(BF16)
        o = None
        for comp in range(2):
            st = jnp.where(tail_mask, _dot_nt(qc[comp], kt), -jnp.inf)
            m = jnp.max(st, axis=-1, keepdims=True)
            if p0:
                sf = _dot_nt(qc[comp], kf)
                m = jnp.maximum(m, jnp.max(sf, axis=-1, keepdims=True))
            et = jnp.exp(st - m)
            l = jnp.sum(et, axis=-1, keepdims=True)
            acc = _dot(et.astype(BF16), vt)
            if p0:
                ef = jnp.exp(sf - m)
                l = l + jnp.sum(ef, axis=-1, keepdims=True)
                acc = acc + _dot(ef.astype(BF16), vf)
            oc = acc / l
            o = oc if comp == 0 else o - lam * oc
        o_ref[0] = _rms(o, ng_ref[...]) * (1.0 - lam_init)

    for i in range(nq):
        pl.when(qi == i)(functools.partial(tile, i))


def diff_attn(q_src, kv_src, q_col0, k_col0, v_col0, lam_vecs, norm_g, past, lam_init):
    b, t_q, _ = q_src.shape
    t_k = kv_src.shape[1]
    tq = _row_tile(t_q, 256)
    nq = t_q // tq
    assert t_k == past + t_q and past % CHUNK == 0 and past % 8 == 0
    assert nq == 1 or tq % CHUNK == 0
    return pl.pallas_call(
        functools.partial(_attn_kernel, tq=tq, nq=nq, past=past, lam_init=lam_init),
        grid=(b, C_HEADS, t_q // tq),
        in_specs=[
            pl.BlockSpec((1, tq, C_VD), lambda i, h, j: (i, j, q_col0 + h)),
            pl.BlockSpec((1, t_k, C_VD), lambda i, h, j: (i, 0, k_col0 + h)),
            pl.BlockSpec((1, t_k, C_VD), lambda i, h, j: (i, 0, v_col0 + h)),
            pl.BlockSpec((4, C_DH), lambda i, h, j: (0, 0)),
            pl.BlockSpec((1, C_VD), lambda i, h, j: (0, 0)),
        ],
        out_specs=pl.BlockSpec((1, tq, C_VD), lambda i, h, j: (i, j, h)),
        out_shape=jax.ShapeDtypeStruct((b, t_q, C_WIDTH), F32),
        compiler_params=_cparams(("parallel", "parallel", "arbitrary")),
        name="diff_attn",
    )(q_src, kv_src, kv_src, lam_vecs, norm_g.reshape(1, C_VD))


def _topk_rows(vals, k, payload=None):
    n_rows = vals.shape[0]
    rio = lax.broadcasted_iota(jnp.int32, vals.shape, 0)
    out_v, out_p = [], []
    for _ in range(k):
        m = jnp.max(vals, axis=0, keepdims=True)
        am = jnp.min(jnp.where(vals == m, rio, n_rows), axis=0, keepdims=True)
        hit = rio == am
        out_v.append(m)
        out_p.append(am if payload is None else jnp.max(jnp.where(hit, payload, -1), axis=0, keepdims=True))
        vals = jnp.where(hit, -jnp.inf, vals)
    return jnp.concatenate(out_v, axis=0), jnp.concatenate(out_p, axis=0)


_CAND_PER_ROW = [PEER_TOPK // (i + 1) for i in range(PEER_TOPK)]
_N_CAND = sum(_CAND_PER_ROW)
_N_CAND_PAD = -(-_N_CAND // 8) * 8


def _peer_topk_kernel(q_ref, keys_ref, idx_ref, gate_ref, idx_t, gate_t, cs_ref, ce_ref):
    tt = q_ref.shape[0]
    half = PEER_DQ // 2
    cs_ref[_N_CAND:, :] = jnp.full((_N_CAND_PAD - _N_CAND, tt), -jnp.inf, F32)
    ce_ref[_N_CAND:, :] = jnp.full((_N_CAND_PAD - _N_CAND, tt), -1, jnp.int32)

    def head(h, carry):
        tops = []
        for p in range(2):
            off = pl.multiple_of(h * PEER_DQ + p * half, half)
            q = q_ref[:, pl.ds(off, half)]
            sc = _dot_nt(keys_ref[p], q, HI)
            tops.append(_topk_rows(sc, PEER_TOPK))
        (s1, i1), (s2, i2) = tops
        row0 = 0
        for i, cnt in enumerate(_CAND_PER_ROW):
            cs_ref[row0:row0 + cnt, :] = s1[i:i + 1] + s2[0:cnt]
            ce_ref[row0:row0 + cnt, :] = i1[i:i + 1] * N_KEYS + i2[0:cnt]
            row0 += cnt
        top_s, top_e = _topk_rows(cs_ref[...], PEER_TOPK, ce_ref[...])
        e = jnp.exp(top_s - top_s[0:1])
        gate = e / jnp.sum(e, axis=0, keepdims=True)
        r0 = pl.multiple_of(h * PEER_TOPK, PEER_TOPK)
        idx_t[pl.ds(r0, PEER_TOPK), :] = top_e
        gate_t[pl.ds(r0, PEER_TOPK), :] = gate
        return carry

    lax.fori_loop(0, PEER_HEADS, head, 0)
    idx_ref[...] = idx_t[...].T
    gate_ref[...] = gate_t[...].T


def peer_topk(qry, sub_keys):
    n = qry.shape[0]
    tt = _row_tile(n, 256)
    nsel = PEER_HEADS * PEER_TOPK
    return pl.pallas_call(
        _peer_topk_kernel,
        grid=(n // tt,),
        in_specs=[
            pl.BlockSpec((tt, PEER_HEADS * PEER_DQ), lambda i: (i, 0)),
            pl.BlockSpec((2, N_KEYS, PEER_DQ // 2), lambda i: (0, 0, 0)),
        ],
        out_specs=[
            pl.BlockSpec((tt, nsel), lambda i: (i, 0)),
            pl.BlockSpec((tt, nsel), lambda i: (i, 0)),
        ],
        out_shape=[
            jax.ShapeDtypeStruct((n, nsel), jnp.int32),
            jax.ShapeDtypeStruct((n, nsel), F32),
        ],
        scratch_shapes=[pltpu.VMEM((nsel, tt), jnp.int32), pltpu.VMEM((nsel, tt), F32),
                        pltpu.VMEM((_N_CAND_PAD, tt), F32), pltpu.VMEM((_N_CAND_PAD, tt), jnp.int32)],
        compiler_params=_cparams(("parallel",)),
        name="peer_topk",
    )(qry, sub_keys)


PEER_GROUP = 8
NSEL = PEER_HEADS * PEER_TOPK
SLAB_ROWS = 8
SLAB_PITCH = 12


def peer_table(u, v):
    ne, d = u.shape
    assert d == SLAB_ROWS * LANES
    hi = lax.bitcast_convert_type(u.astype(BF16), jnp.uint16).astype(jnp.uint32)
    lo = lax.bitcast_convert_type(v.astype(BF16), jnp.uint16).astype(jnp.uint32)
    return ((hi << 16) | lo).reshape(ne, SLAB_ROWS, LANES)


def _peer_expert_kernel(idx_ref, idxn_ref, x_ref, gate_ref, g_ref, gf_ref, uv_hbm, o_ref, gbuf0, gbuf1, sem,
                        *, final_norm):
    g = PEER_GROUP
    d = x_ref.shape[1]
    nsub = d // LANES
    step = pl.program_id(0)
    gbufs = (gbuf0, gbuf1)

    def slab_copy(e, slot, t, r):
        row0 = (t * NSEL + r) * SLAB_PITCH
        return pltpu.make_async_copy(uv_hbm.at[e], gbufs[slot].at[pl.ds(row0, SLAB_ROWS), :], sem.at[slot])

    def issue_token(iref, t_src, slot, t):
        for r in range(NSEL):
            slab_copy(iref[t_src, r], slot, t, r).start(priority=r % 2)

    def wait_group(slot):
        def body(t, c):
            for r in range(NSEL):
                slab_copy(0, slot, 0, 0).wait()
            return c
        lax.fori_loop(0, g, body, 0)

    def expert_rows(slot, t):
        base = t * NSEL * SLAB_PITCH
        words = jnp.concatenate(
            [gbufs[slot][pl.ds(base + s, NSEL, stride=SLAB_PITCH), :] for s in range(nsub)], axis=1)
        u = lax.bitcast_convert_type(words & jnp.uint32(0xFFFF0000), F32).astype(BF16)
        v = lax.bitcast_convert_type(words << 16, F32).astype(BF16)
        return u, v

    def combine_group(row0, slot, iref, t_src0, slot_next):
        x = x_ref[row0:row0 + g, :]
        xnb = _rms(x, g_ref[...]).astype(BF16)
        gate = gate_ref[row0:row0 + g, :]
        rio = lax.broadcasted_iota(jnp.int32, (g, NSEL), 0)
        out = x
        for t in range(g):
            issue_token(iref, t_src0 + t, slot_next, t)
            u, v = expert_rows(slot, t)
            act = _dot_nt(xnb, u)
            w = jnp.where(rio == t, gate * jax.nn.gelu(act), 0.0).astype(BF16)
            out = out + _dot(w, v)
        if final_norm:
            out = _rms(out, gf_ref[...])
        o_ref[row0:row0 + g, :] = out

    @pl.when(step == 0)
    def _():
        for t in range(g):
            issue_token(idx_ref, t, 0, t)

    wait_group(0)
    combine_group(0, 0, idx_ref, g, 1)
    wait_group(1)
    combine_group(g, 1, idxn_ref, 0, 0)

    @pl.when(step == pl.num_programs(0) - 1)
    def _():
        wait_group(0)


def peer_experts(x, idx, gate, norm_g, final_g, uv, final_norm):
    n, d = x.shape
    g = PEER_GROUP
    nsteps = n // (2 * g)
    ngroups = 2 * nsteps
    return pl.pallas_call(
        functools.partial(_peer_expert_kernel, final_norm=final_norm),
        grid=(nsteps,),
        in_specs=[
            pl.BlockSpec((2 * g, NSEL), lambda i: (i, 0), memory_space=pltpu.SMEM),
            pl.BlockSpec((g, NSEL), lambda i: (jnp.minimum(2 * i + 2, ngroups - 1), 0), memory_space=pltpu.SMEM),
            pl.BlockSpec((2 * g, d), lambda i: (i, 0)),
            pl.BlockSpec((2 * g, NSEL), lambda i: (i, 0)),
            pl.BlockSpec((1, d), lambda i: (0, 0)),
            pl.BlockSpec((1, d), lambda i: (0, 0)),
            pl.BlockSpec(memory_space=pl.ANY),
        ],
        out_specs=pl.BlockSpec((2 * g, d), lambda i: (i, 0)),
        out_shape=jax.ShapeDtypeStruct((n, d), F32),
        scratch_shapes=[
            pltpu.VMEM((g * NSEL * SLAB_PITCH, LANES), jnp.uint32),
            pltpu.VMEM((g * NSEL * SLAB_PITCH, LANES), jnp.uint32),
            pltpu.SemaphoreType.DMA((2,)),
        ],
        compiler_params=_cparams(("arbitrary",)),
        name="peer_experts",
    )(idx, idx, x, gate, norm_g.reshape(1, d), final_g.reshape(1, d), uv)


def _block_diag(w):
    nb, bi, bj = w.shape
    out = jnp.zeros((nb * bi, nb * bj), w.dtype)
    for i in range(nb):
        out = out.at[i * bi:(i + 1) * bi, i * bj:(i + 1) * bj].set(w[i])
    return out


def _prep_weights(p):
    depth = p['norm_mix'].shape[0]
    d = p['norm_final'].shape[0]
    w = {}
    a3, a4 = 3 * A_WIDTH, 4 * A_WIDTH
    w['ab_in'], w['ab_out_a'], w['ab_out_b'], w['ga'], w['gx'] = [], [], [], [], []
    for j in range(p['ab_w_in'].shape[0]):
        wi = p['ab_w_in'][j]
        pad = jnp.zeros((d, LANES - 2 * A_HEADS), wi.dtype)
        cols = [wi[:, :a4], wi[:, a4 + 2 * A_HEADS:], wi[:, a4:a4 + 2 * A_HEADS], pad]
        w['ab_in'].append(jnp.concatenate(cols, axis=1).astype(BF16))
        w['ab_out_a'].append(p['ab_w_out'][j][:A_WIDTH].astype(BF16))
        w['ab_out_b'].append(p['ab_w_out'][j][A_WIDTH:].astype(BF16))
        w['ga'].append(_block_diag(p['b_gate_a_w'][j]).astype(BF16))
        w['gx'].append(_block_diag(p['b_gate_x_w'][j]).astype(BF16))
    w['c_qkv'] = [m.astype(BF16) for m in p['c_w_qkv']]
    w['c_out'] = [m.astype(BF16) for m in p['c_w_out']]
    w['c_lam'] = [jnp.stack([p['c_lambda_q1'][j], p['c_lambda_k1'][j], p['c_lambda_q2'][j], p['c_lambda_k2'][j]])
                  for j in range(p['c_w_qkv'].shape[0])]
    w['peer_q'] = [p['peer_w_query'][l].astype(BF16) for l in range(depth)]
    w['peer_uv'] = [peer_table(p['peer_u'][l], p['peer_v'][l]) for l in range(depth)]
    return w


def _peer(x2, l, p, w, final_norm):
    qry = norm_matmul(x2, p['norm_ffn'][l], w['peer_q'][l])
    idx, gate = peer_topk(qry, p['peer_sub_keys'][l])
    return peer_experts(x2, idx, gate, p['norm_ffn'][l], p['norm_final'], w['peer_uv'][l], final_norm)


def _trunk(x, st_a_conv, st_a_delta, st_b_conv, st_b_h, c_k, c_v, p, w, past):
    b, t, d = x.shape
    n = b * t
    chunk = min(CHUNK, t)
    depth = p['norm_mix'].shape[0]
    x2 = x.reshape(n, d)
    a_conv, a_delta, b_conv, b_h, new_k, new_v = [], [], [], [], [], []
    for l in range(depth):
        j = l // 2
        if l % 2 == 0:
            proj = norm_matmul(x2, p['norm_mix'][l], w['ab_in'][j]).reshape(b, t, -1)
            o_a, s_new = gdn(proj, st_a_conv[j], st_a_delta[j], p['a_conv_w'][j], p['a_log'][j],
                             p['a_dt_bias'][j], p['a_norm'][j], chunk)
            o_b, h_last = lru(proj, st_b_conv[j], st_b_h[j], p['b_conv_w'][j], p['b_conv_b'][j], w['ga'][j],
                              p['b_gate_a_b'][j], w['gx'][j], p['b_gate_x_b'][j], p['b_lambda'][j])
            a3 = 3 * A_WIDTH
            xb0 = a3 + A_WIDTH
            a_conv.append(jnp.concatenate([st_a_conv[j], proj[:, :, :a3]], axis=1)[:, -(CONV_W - 1):])
            b_conv.append(jnp.concatenate([st_b_conv[j], proj[:, :, xb0:xb0 + B_WIDTH]], axis=1)[:, -(CONV_W - 1):])
            a_delta.append(s_new)
            b_h.append(h_last)
            x2 = matmul_res([o_a.reshape(n, A_WIDTH), o_b.reshape(n, B_WIDTH)],
                            [w['ab_out_a'][j], w['ab_out_b'][j]], x2)
        else:
            lam_init = 0.8 - 0.6 * math.exp(-0.3 * l)
            qkv = norm_matmul(x2, p['norm_mix'][l], w['c_qkv'][j]).reshape(b, t, 3 * C_WIDTH)
            kn = qkv[:, :, C_WIDTH:2 * C_WIDTH]
            vn = qkv[:, :, 2 * C_WIDTH:]
            if c_k is None:
                o_c = diff_attn(qkv, qkv, 0, C_HEADS, 2 * C_HEADS, w['c_lam'][j], p['c_norm'][j], past, lam_init)
            else:
                k_all = jnp.concatenate([c_k[j].reshape(b, -1, C_WIDTH), kn], axis=1)
                v_all = jnp.concatenate([c_v[j].reshape(b, -1, C_WIDTH), vn], axis=1)
                kv = jnp.concatenate([k_all, v_all], axis=2)
                o_c = diff_attn(qkv, kv, 0, 0, C_HEADS, w['c_lam'][j], p['c_norm'][j], past, lam_init)
            new_k.append(kn.reshape(b, t, C_HEADS, 2, C_DH))
            new_v.append(vn.reshape(b, t, C_HEADS, C_VD))
            x2 = matmul_res([o_c.reshape(n, C_WIDTH)], [w['c_out'][j]], x2)
        x2 = _peer(x2, l, p, w, final_norm=(l == depth - 1))
    return (x2.reshape(b, t, d), jnp.stack(a_conv), jnp.stack(a_delta), jnp.stack(b_conv), jnp.stack(b_h),
            jnp.stack(new_k), jnp.stack(new_v))


def kernel(x_prompt, x_sample, state_a_conv, state_a_delta, state_b_conv, state_b_rglru, cache_c_k, cache_c_v, norm_mix, norm_ffn, norm_final, ab_w_in, ab_w_out, a_conv_w, a_log, a_dt_bias, a_norm, b_conv_w, b_conv_b, b_gate_a_w, b_gate_a_b, b_gate_x_w, b_gate_x_b, b_lambda, c_w_qkv, c_w_out, c_lambda_q1, c_lambda_k1, c_lambda_q2, c_lambda_k2, c_norm, peer_w_query, peer_sub_keys, peer_u, peer_v):
    p = dict(norm_mix=norm_mix, norm_ffn=norm_ffn, norm_final=norm_final, ab_w_in=ab_w_in,
             ab_w_out=ab_w_out, a_conv_w=a_conv_w, a_log=a_log, a_dt_bias=a_dt_bias, a_norm=a_norm,
             b_conv_w=b_conv_w, b_conv_b=b_conv_b, b_gate_a_w=b_gate_a_w, b_gate_a_b=b_gate_a_b,
             b_gate_x_w=b_gate_x_w, b_gate_x_b=b_gate_x_b, b_lambda=b_lambda, c_w_qkv=c_w_qkv,
             c_w_out=c_w_out, c_lambda_q1=c_lambda_q1, c_lambda_k1=c_lambda_k1,
             c_lambda_q2=c_lambda_q2, c_lambda_k2=c_lambda_k2, c_norm=c_norm,
             peer_w_query=peer_w_query, peer_sub_keys=peer_sub_keys, peer_u=peer_u, peer_v=peer_v)
    w = _prep_weights(p)
    bp = x_prompt.shape[0]
    dt = x_prompt.dtype
    n_even = state_a_conv.shape[0]
    z_a_conv = jnp.zeros((n_even, bp) + state_a_conv.shape[2:], dt)
    z_a_delta = jnp.zeros((n_even, bp) + state_a_delta.shape[2:], dt)
    z_b_conv = jnp.zeros((n_even, bp) + state_b_conv.shape[2:], dt)
    z_b_h = jnp.zeros((n_even, bp) + state_b_rglru.shape[2:], dt)
    (y_prompt, pa_conv, pa_delta, pb_conv, pb_h, pc_k, pc_v) = _trunk(
        x_prompt, z_a_conv, z_a_delta, z_b_conv, z_b_h, None, None, p, w, 0)
    (y_sample, sa_conv, sa_delta, sb_conv, sb_h, sc_k, sc_v) = _trunk(
        x_sample, state_a_conv, state_a_delta, state_b_conv, state_b_rglru, cache_c_k, cache_c_v,
        p, w, cache_c_k.shape[2])
    return (y_prompt, y_sample, pa_conv, pa_delta, pb_conv, pb_h, pc_k, pc_v,
            sa_conv, sa_delta, sb_conv, sb_h, sc_k, sc_v)
```

```python
import functools
import math

import jax
import jax.numpy as jnp
from jax import lax
from jax.experimental import pallas as pl
from jax.experimental.pallas import tpu as pltpu

F32 = jnp.float32
BF16 = jnp.bfloat16
HI = lax.Precision.HIGHEST

EPS = 1e-6
CHUNK = 64
CONV_W = 4
A_HEADS = 4
A_DK = 128
A_WIDTH = A_HEADS * A_DK
B_WIDTH = 512
B_BLOCKS = 8
RG_C = 8.0
C_HEADS = 8
C_DH = 64
C_VD = 2 * C_DH
C_WIDTH = C_HEADS * C_VD
PEER_HEADS = 8
N_KEYS = 128
PEER_TOPK = 16
PEER_DQ = 256
LANES = 128
VMEM_LIMIT = 48 * 1024 * 1024


def _cparams(sem):
    return pltpu.CompilerParams(dimension_semantics=sem, vmem_limit_bytes=VMEM_LIMIT)


def _dot(a, b, prec=None):
    return jnp.dot(a, b, precision=prec, preferred_element_type=F32)


def _dot_nt(a, b, prec=None):
    return lax.dot_general(a, b, (((1,), (1,)), ((), ())), precision=prec, preferred_element_type=F32)


def _dot_bf16(a, b):
    return _dot(a.astype(BF16), b.astype(BF16))


def _dot_sel(sel, b, nt=False):
    f = _dot_nt if nt else _dot
    s = sel.astype(F32).astype(BF16)
    b1 = b.astype(BF16)
    r1 = b - b1.astype(F32)
    b2 = r1.astype(BF16)
    b3 = (r1 - b2.astype(F32)).astype(BF16)
    return f(s, b1) + f(s, b2) + f(s, b3)


def _rms(x, g):
    return x * lax.rsqrt(jnp.mean(x * x, axis=-1, keepdims=True) + EPS) * g


def _softplus(x):
    return jnp.maximum(x, 0.0) + jnp.log1p(jnp.exp(-jnp.abs(x)))


def _row_tile(n, pref):
    t = min(n, pref)
    while n % t:
        t //= 2
    return t


def _norm_matmul_kernel(x_ref, g_ref, w_ref, o_ref):
    y = _rms(x_ref[...], g_ref[...])
    o_ref[...] = _dot(y.astype(BF16), w_ref[...])


def norm_matmul(x, g, w):
    n, d = x.shape
    m = w.shape[1]
    tm = _row_tile(n, 256)
    return pl.pallas_call(
        _norm_matmul_kernel,
        grid=(n // tm,),
        in_specs=[
            pl.BlockSpec((tm, d), lambda i: (i, 0)),
            pl.BlockSpec((1, d), lambda i: (0, 0)),
            pl.BlockSpec((d, m), lambda i: (0, 0)),
        ],
        out_specs=pl.BlockSpec((tm, m), lambda i: (i, 0)),
        out_shape=jax.ShapeDtypeStruct((n, m), F32),
        compiler_params=_cparams(("parallel",)),
        name="norm_matmul",
    )(x, g.reshape(1, d), w)


def _matmul_res_kernel(*refs, n_in):
    ys = refs[:n_in]
    ws = refs[n_in:2 * n_in]
    res_ref = refs[2 * n_in]
    o_ref = refs[2 * n_in + 1]
    acc = res_ref[...]
    for y_ref, w_ref in zip(ys, ws):
        acc = acc + _dot(y_ref[...].astype(BF16), w_ref[...])
    o_ref[...] = acc


def matmul_res(ys, ws, res):
    n, d = res.shape
    tm = _row_tile(n, 512)
    n_in = len(ys)
    in_specs = [pl.BlockSpec((tm, y.shape[1]), lambda i: (i, 0)) for y in ys]
    in_specs += [pl.BlockSpec(w.shape, lambda i: (0, 0)) for w in ws]
    in_specs += [pl.BlockSpec((tm, d), lambda i: (i, 0))]
    return pl.pallas_call(
        functools.partial(_matmul_res_kernel, n_in=n_in),
        grid=(n // tm,),
        in_specs=in_specs,
        out_specs=pl.BlockSpec((tm, d), lambda i: (i, 0)),
        out_shape=jax.ShapeDtypeStruct((n, d), F32),
        compiler_params=_cparams(("parallel",)),
        name="matmul_res",
    )(*ys, *ws, res)


def _gdn_kernel(qkv_ref, z_ref, ba_ref, cprev_ref, s0_ref, cw_ref, prm_ref, an_ref,
                o_ref, sout_ref, xp_ref, s_ref, *, chunk):
    c = chunk
    nh = A_HEADS
    r = nh * c
    step = pl.program_id(1)

    @pl.when(step == 0)
    def _():
        xp_ref[5:8, :] = cprev_ref[0]
        for h in range(nh):
            s_ref[:, h * A_DK:(h + 1) * A_DK] = s0_ref[0, h]

    xp_ref[8:8 + c, :] = qkv_ref[0]
    conv = xp_ref[5:5 + c, :] * cw_ref[0:1, :]
    for j in range(1, CONV_W):
        conv = conv + xp_ref[5 + j:5 + j + c, :] * cw_ref[j:j + 1, :]
    xp_ref[5:8, :] = xp_ref[5 + c:8 + c, :]
    act = conv * jax.nn.sigmoid(conv)

    def stack(off):
        return jnp.concatenate([act[:, off + h * A_DK: off + (h + 1) * A_DK] for h in range(nh)], axis=0)

    qs, ks, vs = stack(0), stack(A_WIDTH), stack(2 * A_WIDTH)
    qs = qs * lax.rsqrt(jnp.sum(qs * qs, axis=-1, keepdims=True) + EPS) * (A_DK ** -0.5)
    ks = ks * lax.rsqrt(jnp.sum(ks * ks, axis=-1, keepdims=True) + EPS)

    ba = ba_ref[0]
    beta_full = jax.nn.sigmoid(ba)
    g_full = -jnp.exp(prm_ref[0:1, :]) * _softplus(ba + prm_ref[1:2, :])
    beta_s = jnp.concatenate([beta_full[:, h:h + 1] for h in range(nh)], axis=0)
    g_s = jnp.concatenate([g_full[:, nh + h:nh + h + 1] for h in range(nh)], axis=0)

    row = lax.broadcasted_iota(jnp.int32, (r, r), 0)
    col = lax.broadcasted_iota(jnp.int32, (r, r), 1)
    same = (row // c) == (col // c)
    m_incl = same & (col <= row)
    m_strict = same & (col < row)

    gcs = _dot_sel(m_incl, jnp.broadcast_to(g_s, (r, LANES)))
    lane0 = lax.broadcasted_iota(jnp.int32, (r, LANES), 1) == 0
    gcs_row = _dot_sel(lane0, gcs, nt=True)
    gc = gcs[:, 0:1]
    decay = jnp.exp(jnp.where(m_incl, gc - gcs_row, -jnp.inf))

    kb = ks * beta_s
    ks_b = ks.astype(BF16)
    a_mat = jnp.where(m_strict, _dot_nt(kb.astype(BF16), ks_b), 0.0) * decay
    eye = (row == col).astype(F32)
    p = -a_mat
    t_inv = eye + p
    for _ in range(int(math.log2(c)) - 1):
        p = _dot(p, p, HI)
        t_inv = t_inv + _dot(t_inv, p, HI)
    uw = _dot(t_inv, jnp.concatenate([vs * beta_s, kb * jnp.exp(gc)], axis=1), HI)
    u, w = uw[:, :A_DK], uw[:, A_DK:]
    qk = jnp.where(m_incl, _dot_nt(qs.astype(BF16), ks_b), 0.0) * decay

    def diag_blocks(x):
        return jnp.concatenate([x[h * c:(h + 1) * c, h * A_DK:(h + 1) * A_DK] for h in range(nh)], axis=0)

    s_cat = s_ref[...]
    wq_s = _dot_bf16(jnp.concatenate([w, qs * jnp.exp(gc)], axis=0), s_cat)
    v_new = u - diag_blocks(wq_s[:r])
    o_s = diag_blocks(wq_s[r:]) + _dot_bf16(qk, v_new)

    g_last = jnp.concatenate(
        [jnp.broadcast_to(gcs[(h + 1) * c - 1:(h + 1) * c, :], (c, LANES)) for h in range(nh)], axis=0)
    kdec = ks * jnp.exp(g_last - gcs)
    zero = jnp.zeros((c, A_DK), F32)
    v_exp = jnp.concatenate(
        [jnp.concatenate([v_new[h * c:(h + 1) * c] if hh == h else zero for hh in range(nh)], axis=1)
         for h in range(nh)], axis=0)
    s_scale = jnp.concatenate(
        [jnp.exp(jnp.broadcast_to(gcs[(h + 1) * c - 1:(h + 1) * c, :], (A_DK, LANES))) for h in range(nh)], axis=1)
    s_new = s_cat * s_scale + _dot_bf16(kdec.T, v_exp)
    s_ref[...] = s_new

    o_n = _rms(o_s, an_ref[...])
    o_full = jnp.concatenate([o_n[h * c:(h + 1) * c] for h in range(nh)], axis=1)
    zz = z_ref[0]
    o_ref[0] = o_full * (zz * jax.nn.sigmoid(zz))

    @pl.when(step == pl.num_programs(1) - 1)
    def _():
        for h in range(nh):
            sout_ref[0, h] = s_new[:, h * A_DK:(h + 1) * A_DK]


def gdn(proj, conv_prev, s0, conv_w, a_log, dt_bias, a_norm, chunk):
    b, t, _ = proj.shape
    nsteps = t // chunk
    prm = jnp.zeros((2, LANES), F32)
    prm = prm.at[0, A_HEADS:2 * A_HEADS].set(a_log).at[1, A_HEADS:2 * A_HEADS].set(dt_bias)
    qkv_w = 3 * A_WIDTH
    return pl.pallas_call(
        functools.partial(_gdn_kernel, chunk=chunk),
        grid=(b, nsteps),
        in_specs=[
            pl.BlockSpec((1, chunk, qkv_w), lambda i, j: (i, j, 0)),
            pl.BlockSpec((1, chunk, A_WIDTH), lambda i, j: (i, j, qkv_w // A_WIDTH)),
            pl.BlockSpec((1, chunk, LANES), lambda i, j: (i, j, (qkv_w + 3 * A_WIDTH) // LANES)),
            pl.BlockSpec((1, CONV_W - 1, qkv_w), lambda i, j: (i, 0, 0)),
            pl.BlockSpec((1, A_HEADS, A_DK, A_DK), lambda i, j: (i, 0, 0, 0)),
            pl.BlockSpec((CONV_W, qkv_w), lambda i, j: (0, 0)),
            pl.BlockSpec((2, LANES), lambda i, j: (0, 0)),
            pl.BlockSpec((1, A_DK), lambda i, j: (0, 0)),
        ],
        out_specs=[
            pl.BlockSpec((1, chunk, A_WIDTH), lambda i, j: (i, j, 0)),
            pl.BlockSpec((1, A_HEADS, A_DK, A_DK), lambda i, j: (i, 0, 0, 0)),
        ],
        out_shape=[
            jax.ShapeDtypeStruct((b, t, A_WIDTH), F32),
            jax.ShapeDtypeStruct((b, A_HEADS, A_DK, A_DK), F32),
        ],
        scratch_shapes=[
            pltpu.VMEM((chunk + 8, qkv_w), F32),
            pltpu.VMEM((A_DK, A_HEADS * A_DK), F32),
        ],
        compiler_params=_cparams(("parallel", "arbitrary")),
        name="gdn",
    )(proj, proj, proj, conv_prev, s0, conv_w, prm, a_norm.reshape(1, A_DK))


def _lru_kernel(xb_ref, gb_ref, cprev_ref, h0_ref, cw_ref, cb_ref, wa_ref, ba_ref, wx_ref, bx_ref, lam_ref,
                ob_ref, hl_ref, xp_ref, a_ref, u_ref, hs_ref, h_ref, *, tb):
    step = pl.program_id(1)

    @pl.when(step == 0)
    def _():
        xp_ref[5:8, :] = cprev_ref[0]
        h_ref[...] = h0_ref[0]

    xp_ref[8:8 + tb, :] = xb_ref[0]
    xc = xp_ref[5:5 + tb, :] * cw_ref[0:1, :]
    for j in range(1, CONV_W):
        xc = xc + xp_ref[5 + j:5 + j + tb, :] * cw_ref[j:j + 1, :]
    xp_ref[5:8, :] = xp_ref[5 + tb:8 + tb, :]
    xc = xc + cb_ref[...]
    xcb = xc.astype(BF16)
    rg = jax.nn.sigmoid(_dot(xcb, wa_ref[...]) + ba_ref[...])
    ig = jax.nn.sigmoid(_dot(xcb, wx_ref[...]) + bx_ref[...])
    log_a = -RG_C * rg * _softplus(-lam_ref[...])
    a_ref[...] = jnp.exp(log_a)
    u_ref[...] = jnp.sqrt(1.0 - jnp.exp(2.0 * log_a)) * (ig * xc)

    def body(i, h):
        h = a_ref[pl.ds(i, 1), :] * h + u_ref[pl.ds(i, 1), :]
        hs_ref[pl.ds(i, 1), :] = h
        return h

    h = lax.fori_loop(0, tb, body, h_ref[...], unroll=8)
    h_ref[...] = h
    ob_ref[0] = hs_ref[...] * jax.nn.gelu(gb_ref[0])

    @pl.when(step == pl.num_programs(1) - 1)
    def _():
        hl_ref[0] = h


def lru(proj, conv_prev, h0, conv_w, conv_b, wa_bd, ba, wx_bd, bx, lam):
    b, t, _ = proj.shape
    tb = _row_tile(t, 256)
    xb_blk = (3 * A_WIDTH + A_WIDTH) // B_WIDTH
    vec = lambda v: v.reshape(1, B_WIDTH)
    const = lambda shape: pl.BlockSpec(shape, lambda i, j: (0,) * len(shape))
    ob, hl = pl.pallas_call(
        functools.partial(_lru_kernel, tb=tb),
        grid=(b, t // tb),
        in_specs=[
            pl.BlockSpec((1, tb, B_WIDTH), lambda i, j: (i, j, xb_blk)),
            pl.BlockSpec((1, tb, B_WIDTH), lambda i, j: (i, j, xb_blk + 1)),
            pl.BlockSpec((1, CONV_W - 1, B_WIDTH), lambda i, j: (i, 0, 0)),
            pl.BlockSpec((1, 1, B_WIDTH), lambda i, j: (i, 0, 0)),
            const((CONV_W, B_WIDTH)), const((1, B_WIDTH)),
            const((B_WIDTH, B_WIDTH)), const((1, B_WIDTH)),
            const((B_WIDTH, B_WIDTH)), const((1, B_WIDTH)),
            const((1, B_WIDTH)),
        ],
        out_specs=[
            pl.BlockSpec((1, tb, B_WIDTH), lambda i, j: (i, j, 0)),
            pl.BlockSpec((1, 1, B_WIDTH), lambda i, j: (i, 0, 0)),
        ],
        out_shape=[
            jax.ShapeDtypeStruct((b, t, B_WIDTH), F32),
            jax.ShapeDtypeStruct((b, 1, B_WIDTH), F32),
        ],
        scratch_shapes=[
            pltpu.VMEM((tb + 8, B_WIDTH), F32),
            pltpu.VMEM((tb, B_WIDTH), F32),
            pltpu.VMEM((tb, B_WIDTH), F32),
            pltpu.VMEM((tb, B_WIDTH), F32),
            pltpu.VMEM((1, B_WIDTH), F32),
        ],
        compiler_params=_cparams(("parallel", "arbitrary")),
        name="lru",
    )(proj, proj, conv_prev, h0.reshape(b, 1, B_WIDTH), conv_w, vec(conv_b), wa_bd, vec(ba), wx_bd, vec(bx), vec(lam))
    return ob, hl.reshape(b, B_WIDTH)


def _attn_kernel(q_ref, k_ref, v_ref, lam_ref, ng_ref, o_ref, *, tq, nq, past, lam_init):
    qi = pl.program_id(2)
    q = q_ref[0] * (C_DH ** -0.5)
    lane = lax.broadcasted_iota(jnp.int32, q.shape, 1)
    qc = (jnp.where(lane < C_DH, q, 0.0).astype(BF16), jnp.where(lane >= C_DH, q, 0.0).astype(BF16))
    lv = lam_ref[...]
    lam = (jnp.exp(jnp.sum(lv[0:1] * lv[1:2], axis=-1, keepdims=True))
           - jnp.exp(jnp.sum(lv[2:3] * lv[3:4], axis=-1, keepdims=True)) + lam_init)
    rpos = lax.broadcasted_iota(jnp.int32, (tq, tq), 0)
    cpos = lax.broadcasted_iota(jnp.int32, (tq, tq), 1)
    tail_mask = (cpos // CHUNK) <= (rpos // CHUNK)

    def tile(i):
        p0 = past + i * tq
        kt = k_ref[0, p0:p0 + tq, :].astype(BF16)
        vt = v_ref[0, p0:p0 + tq, :].astype(BF16)
        if p0:
            kf = k_ref[0, 0:p0, :].astype(BF16)
            vf = v_ref[0, 0:p0, :].astype(BF16)
        o = None
        for comp in range(2):
            st = jnp.where(tail_mask, _dot_nt(qc[comp], kt), -jnp.inf)
            m = jnp.max(st, axis=-1, keepdims=True)
            if p0:
                sf = _dot_nt(qc[comp], kf)
                m = jnp.maximum(m, jnp.max(sf, axis=-1, keepdims=True))
            et = jnp.exp(st - m)
            l = jnp.sum(et, axis=-1, keepdims=True)
            acc = _dot(et.astype(BF16), vt)
            if p0:
                ef = jnp.exp(sf - m)
                l = l + jnp.sum(ef, axis=-1, keepdims=True)
                acc = acc + _dot(ef.astype(BF16), vf)
            oc = acc / l
            o = oc if comp == 0 else o - lam * oc
        o_ref[0] = _rms(o, ng_ref[...]) * (1.0 - lam_init)

    for i in range(nq):
        pl.when(qi == i)(functools.partial(tile, i))


def diff_attn(q_src, kv_src, q_col0, k_col0, v_col0, lam_vecs, norm_g, past, lam_init):
    b, t_q, _ = q_src.shape
    t_k = kv_src.shape[1]
    tq = _row_tile(t_q, 256)
    nq = t_q // tq
    assert t_k == past + t_q and past % CHUNK == 0 and past % 8 == 0
    assert nq == 1 or tq % CHUNK == 0
    return pl.pallas_call(
        functools.partial(_attn_kernel, tq=tq, nq=nq, past=past, lam_init=lam_init),
        grid=(b, C_HEADS, t_q // tq),
        in_specs=[
            pl.BlockSpec((1, tq, C_VD), lambda i, h, j: (i, j, q_col0 + h)),
            pl.BlockSpec((1, t_k, C_VD), lambda i, h, j: (i, 0, k_col0 + h)),
            pl.BlockSpec((1, t_k, C_VD), lambda i, h, j: (i, 0, v_col0 + h)),
            pl.BlockSpec((4, C_DH), lambda i, h, j: (0, 0)),
            pl.BlockSpec((1, C_VD), lambda i, h, j: (0, 0)),
        ],
        out_specs=pl.BlockSpec((1, tq, C_VD), lambda i, h, j: (i, j, h)),
        out_shape=jax.ShapeDtypeStruct((b, t_q, C_WIDTH), F32),
        compiler_params=_cparams(("parallel", "parallel", "arbitrary")),
        name="diff_attn",
    )(q_src, kv_src, kv_src, lam_vecs, norm_g.reshape(1, C_VD))


def _topk_rows(vals, k, payload=None):
    n_rows = vals.shape[0]
    rio = lax.broadcasted_iota(jnp.int32, vals.shape, 0)
    out_v, out_p = [], []
    for _ in range(k):
        m = jnp.max(vals, axis=0, keepdims=True)
        am = jnp.min(jnp.where(vals == m, rio, n_rows), axis=0, keepdims=True)
        hit = rio == am
        out_v.append(m)
        out_p.append(am if payload is None else jnp.max(jnp.where(hit, payload, -1), axis=0, keepdims=True))
        vals = jnp.where(hit, -jnp.inf, vals)
    return jnp.concatenate(out_v, axis=0), jnp.concatenate(out_p, axis=0)


_CAND_PER_ROW = [PEER_TOPK // (i + 1) for i in range(PEER_TOPK)]
_N_CAND = sum(_CAND_PER_ROW)
_N_CAND_PAD = -(-_N_CAND // 8) * 8


TOPK_TOKENS = 256


def _subkey_top(q_ref, keys_ref, h, p):
    half = PEER_DQ // 2
    off = pl.multiple_of(h * PEER_DQ + p * half, half)
    sc = _dot_nt(keys_ref[p], q_ref[:, pl.ds(off, half)], HI)
    return _topk_rows(sc, PEER_TOPK)


def _pair_top(s1, i1, s2, i2, h, cs_ref, ce_ref, idx_t, gate_t):
    tt = s1.shape[1]
    cs_ref[_N_CAND:, :] = jnp.full((_N_CAND_PAD - _N_CAND, tt), -jnp.inf, F32)
    ce_ref[_N_CAND:, :] = jnp.full((_N_CAND_PAD - _N_CAND, tt), -1, jnp.int32)
    row0 = 0
    for i, cnt in enumerate(_CAND_PER_ROW):
        cs_ref[row0:row0 + cnt, :] = s1[i:i + 1] + s2[0:cnt]
        ce_ref[row0:row0 + cnt, :] = i1[i:i + 1] * N_KEYS + i2[0:cnt]
        row0 += cnt
    top_s, top_e = _topk_rows(cs_ref[...], PEER_TOPK, ce_ref[...])
    e = jnp.exp(top_s - top_s[0:1])
    gate = e / jnp.sum(e, axis=0, keepdims=True)
    r0 = pl.multiple_of(h * PEER_TOPK, PEER_TOPK)
    idx_t[pl.ds(r0, PEER_TOPK), :] = top_e
    gate_t[pl.ds(r0, PEER_TOPK), :] = gate


def _topk_scratch(tt):
    return [pltpu.VMEM((NSEL, tt), jnp.int32), pltpu.VMEM((NSEL, tt), F32),
            pltpu.VMEM((_N_CAND_PAD, tt), F32), pltpu.VMEM((_N_CAND_PAD, tt), jnp.int32)]


def _peer_topk_kernel(q_ref, keys_ref, idx_ref, gate_ref, idx_t, gate_t, cs_ref, ce_ref):
    def head(h, carry):
        s1, i1 = _subkey_top(q_ref, keys_ref, h, 0)
        s2, i2 = _subkey_top(q_ref, keys_ref, h, 1)
        _pair_top(s1, i1, s2, i2, h, cs_ref, ce_ref, idx_t, gate_t)
        return carry

    lax.fori_loop(0, PEER_HEADS, head, 0)
    idx_ref[...] = idx_t[...].T
    gate_ref[...] = gate_t[...].T


def peer_topk(qry, sub_keys, n_rows):
    n = n_rows
    tt = _row_tile(n, TOPK_TOKENS)
    nsel = NSEL
    return pl.pallas_call(
        _peer_topk_kernel,
        grid=(n // tt,),
        in_specs=[
            pl.BlockSpec((tt, PEER_HEADS * PEER_DQ), lambda i: (i, 0)),
            pl.BlockSpec((2, N_KEYS, PEER_DQ // 2), lambda i: (0, 0, 0)),
        ],
        out_specs=[
            pl.BlockSpec((tt, nsel), lambda i: (i, 0)),
            pl.BlockSpec((tt, nsel), lambda i: (i, 0)),
        ],
        out_shape=[
            jax.ShapeDtypeStruct((n, nsel), jnp.int32),
            jax.ShapeDtypeStruct((n, nsel), F32),
        ],
        scratch_shapes=_topk_scratch(tt),
        compiler_params=_cparams(("parallel",)),
        name="peer_topk",
    )(qry, sub_keys)


PEER_GROUP = 8
NSEL = PEER_HEADS * PEER_TOPK
SLAB_ROWS = 8
SLAB_PITCH = 12


def peer_table(u, v):
    ne, d = u.shape
    assert d == SLAB_ROWS * LANES
    hi = lax.bitcast_convert_type(u.astype(BF16), jnp.uint16).astype(jnp.uint32)
    lo = lax.bitcast_convert_type(v.astype(BF16), jnp.uint16).astype(jnp.uint32)
    return ((hi << 16) | lo).reshape(ne, SLAB_ROWS, LANES)


SIDE_SLICES = 2 * PEER_HEADS


def _peer_expert_kernel(*refs, final_norm, side):
    if side:
        (idx_ref, idxn_ref, x_ref, gate_ref, g_ref, gf_ref, uv_hbm, qn_ref, keys_ref,
         o_ref, idxo_ref, gateo_ref, gbuf0, gbuf1, sem, idx_t, gate_t, cs_ref, ce_ref, s_keep, i_keep) = refs
    else:
        idx_ref, idxn_ref, x_ref, gate_ref, g_ref, gf_ref, uv_hbm, o_ref, gbuf0, gbuf1, sem = refs
    g = PEER_GROUP
    d = x_ref.shape[1]
    nsub = d // LANES
    step = pl.program_id(0)
    gbufs = (gbuf0, gbuf1)

    def side_slice():
        k = step % SIDE_SLICES
        h = k // 2
        p = k % 2
        s, i = _subkey_top(qn_ref, keys_ref, h, p)

        @pl.when(p == 0)
        def _():
            s_keep[...] = s
            i_keep[...] = i

        @pl.when(p == 1)
        def _():
            _pair_top(s_keep[...], i_keep[...], s, i, h, cs_ref, ce_ref, idx_t, gate_t)

        @pl.when(k == SIDE_SLICES - 1)
        def _():
            idxo_ref[...] = idx_t[...].T
            gateo_ref[...] = gate_t[...].T

    def slab_copy(e, slot, t, r):
        row0 = (t * NSEL + r) * SLAB_PITCH
        return pltpu.make_async_copy(uv_hbm.at[e], gbufs[slot].at[pl.ds(row0, SLAB_ROWS), :], sem.at[slot])

    def issue_token(iref, t_src, slot, t):
        for r in range(NSEL):
            slab_copy(iref[t_src, r], slot, t, r).start(priority=r % 2)

    def wait_group(slot):
        def body(t, c):
            for r in range(NSEL):
                slab_copy(0, slot, 0, 0).wait()
            return c
        lax.fori_loop(0, g, body, 0)

    def expert_rows(slot, t):
        base = t * NSEL * SLAB_PITCH
        words = jnp.concatenate(
            [gbufs[slot][pl.ds(base + s, NSEL, stride=SLAB_PITCH), :] for s in range(nsub)], axis=1)
        u = lax.bitcast_convert_type(words & jnp.uint32(0xFFFF0000), F32).astype(BF16)
        v = lax.bitcast_convert_type(words << 16, F32).astype(BF16)
        return u, v

    def combine_group(row0, slot, iref, t_src0, slot_next):
        x = x_ref[row0:row0 + g, :]
        xnb = _rms(x, g_ref[...]).astype(BF16)
        gate = gate_ref[row0:row0 + g, :]
        rio = lax.broadcasted_iota(jnp.int32, (g, NSEL), 0)
        out = x
        for t in range(g):
            issue_token(iref, t_src0 + t, slot_next, t)
            u, v = expert_rows(slot, t)
            act = _dot_nt(xnb, u)
            w = jnp.where(rio == t, gate * jax.nn.gelu(act), 0.0).astype(BF16)
            out = out + _dot(w, v)
        if final_norm:
            out = _rms(out, gf_ref[...])
        o_ref[row0:row0 + g, :] = out

    @pl.when(step == 0)
    def _():
        for t in range(g):
            issue_token(idx_ref, t, 0, t)

    if side:
        side_slice()
    wait_group(0)
    combine_group(0, 0, idx_ref, g, 1)
    wait_group(1)
    combine_group(g, 1, idxn_ref, 0, 0)

    @pl.when(step == pl.num_programs(0) - 1)
    def _():
        wait_group(0)


def peer_experts(x, idx, gate, norm_g, final_g, uv, final_norm, row0=0, side=None):
    n = idx.shape[0]
    d = x.shape[1]
    g = PEER_GROUP
    step_rows = 2 * g
    nsteps = n // step_rows
    ngroups = 2 * nsteps
    assert n % step_rows == 0 and row0 % step_rows == 0
    blk0 = row0 // step_rows
    in_specs = [
        pl.BlockSpec((step_rows, NSEL), lambda i: (i, 0), memory_space=pltpu.SMEM),
        pl.BlockSpec((g, NSEL), lambda i: (jnp.minimum(2 * i + 2, ngroups - 1), 0), memory_space=pltpu.SMEM),
        pl.BlockSpec((step_rows, d), lambda i: (i + blk0, 0)),
        pl.BlockSpec((step_rows, NSEL), lambda i: (i, 0)),
        pl.BlockSpec((1, d), lambda i: (0, 0)),
        pl.BlockSpec((1, d), lambda i: (0, 0)),
        pl.BlockSpec(memory_space=pl.ANY),
    ]
    args = [idx, idx, x, gate, norm_g.reshape(1, d), final_g.reshape(1, d), uv]
    out_specs = [pl.BlockSpec((step_rows, d), lambda i: (i, 0))]
    out_shape = [jax.ShapeDtypeStruct((n, d), F32)]
    scratch = [
        pltpu.VMEM((g * NSEL * SLAB_PITCH, LANES), jnp.uint32),
        pltpu.VMEM((g * NSEL * SLAB_PITCH, LANES), jnp.uint32),
        pltpu.SemaphoreType.DMA((2,)),
    ]
    if side is not None:
        qry, sub_keys, side_row0 = side
        tt = TOPK_TOKENS
        assert tt == SIDE_SLICES * step_rows and n % tt == 0 and side_row0 % tt == 0
        sblk0 = side_row0 // tt
        in_specs += [
            pl.BlockSpec((tt, qry.shape[1]), lambda i: (i // SIDE_SLICES + sblk0, 0)),
            pl.BlockSpec(sub_keys.shape, lambda i: (0, 0, 0)),
        ]
        args += [qry, sub_keys]
        out_specs += [pl.BlockSpec((tt, NSEL), lambda i: (i // SIDE_SLICES, 0)),
                      pl.BlockSpec((tt, NSEL), lambda i: (i // SIDE_SLICES, 0))]
        out_shape += [jax.ShapeDtypeStruct((n, NSEL), jnp.int32), jax.ShapeDtypeStruct((n, NSEL), F32)]
        scratch += _topk_scratch(tt) + [pltpu.VMEM((PEER_TOPK, tt), F32), pltpu.VMEM((PEER_TOPK, tt), jnp.int32)]
    res = pl.pallas_call(
        functools.partial(_peer_expert_kernel, final_norm=final_norm, side=side is not None),
        grid=(nsteps,),
        in_specs=in_specs,
        out_specs=out_specs,
        out_shape=out_shape,
        scratch_shapes=scratch,
        compiler_params=_cparams(("arbitrary",)),
        name="peer_experts",
    )(*args)
    return res if side is not None else res[0]


def _block_diag(w):
    nb, bi, bj = w.shape
    out = jnp.zeros((nb * bi, nb * bj), w.dtype)
    for i in range(nb):
        out = out.at[i * bi:(i + 1) * bi, i * bj:(i + 1) * bj].set(w[i])
    return out


def _prep_weights(p):
    depth = p['norm_mix'].shape[0]
    d = p['norm_final'].shape[0]
    w = {}
    a3, a4 = 3 * A_WIDTH, 4 * A_WIDTH
    w['ab_in'], w['ab_out_a'], w['ab_out_b'], w['ga'], w['gx'] = [], [], [], [], []
    for j in range(p['ab_w_in'].shape[0]):
        wi = p['ab_w_in'][j]
        pad = jnp.zeros((d, LANES - 2 * A_HEADS), wi.dtype)
        cols = [wi[:, :a4], wi[:, a4 + 2 * A_HEADS:], wi[:, a4:a4 + 2 * A_HEADS], pad]
        w['ab_in'].append(jnp.concatenate(cols, axis=1).astype(BF16))
        w['ab_out_a'].append(p['ab_w_out'][j][:A_WIDTH].astype(BF16))
        w['ab_out_b'].append(p['ab_w_out'][j][A_WIDTH:].astype(BF16))
        w['ga'].append(_block_diag(p['b_gate_a_w'][j]).astype(BF16))
        w['gx'].append(_block_diag(p['b_gate_x_w'][j]).astype(BF16))
    w['c_qkv'] = [m.astype(BF16) for m in p['c_w_qkv']]
    w['c_out'] = [m.astype(BF16) for m in p['c_w_out']]
    w['c_lam'] = [jnp.stack([p['c_lambda_q1'][j], p['c_lambda_k1'][j], p['c_lambda_q2'][j], p['c_lambda_k2'][j]])
                  for j in range(p['c_w_qkv'].shape[0])]
    w['peer_q'] = [p['peer_w_query'][l].astype(BF16) for l in range(depth)]
    w['peer_uv'] = [peer_table(p['peer_u'][l], p['peer_v'][l]) for l in range(depth)]
    return w


PEER_PARTS = 4


def _peer(x2, l, p, w, final_norm):
    n = x2.shape[0]
    keys = p['peer_sub_keys'][l]
    qry = norm_matmul(x2, p['norm_ffn'][l], w['peer_q'][l])
    parts = PEER_PARTS if n % (PEER_PARTS * TOPK_TOKENS) == 0 else 1
    n_part = n // parts
    idx, gate = peer_topk(qry, keys, n_part)
    outs = []
    for i in range(parts):
        side = (qry, keys, (i + 1) * n_part) if i + 1 < parts else None
        res = peer_experts(x2, idx, gate, p['norm_ffn'][l], p['norm_final'], w['peer_uv'][l], final_norm,
                           row0=i * n_part, side=side)
        if side is None:
            outs.append(res)
        else:
            out, idx, gate = res
            outs.append(out)
    return outs[0] if parts == 1 else jnp.concatenate(outs, axis=0)


def _trunk(x, st_a_conv, st_a_delta, st_b_conv, st_b_h, c_k, c_v, p, w, past):
    b, t, d = x.shape
    n = b * t
    chunk = min(CHUNK, t)
    depth = p['norm_mix'].shape[0]
    x2 = x.reshape(n, d)
    a_conv, a_delta, b_conv, b_h, new_k, new_v = [], [], [], [], [], []
    for l in range(depth):
        j = l // 2
        if l % 2 == 0:
            proj = norm_matmul(x2, p['norm_mix'][l], w['ab_in'][j]).reshape(b, t, -1)
            o_a, s_new = gdn(proj, st_a_conv[j], st_a_delta[j], p['a_conv_w'][j], p['a_log'][j],
                             p['a_dt_bias'][j], p['a_norm'][j], chunk)
            o_b, h_last = lru(proj, st_b_conv[j], st_b_h[j], p['b_conv_w'][j], p['b_conv_b'][j], w['ga'][j],
                              p['b_gate_a_b'][j], w['gx'][j], p['b_gate_x_b'][j], p['b_lambda'][j])
            a3 = 3 * A_WIDTH
            xb0 = a3 + A_WIDTH
            a_conv.append(jnp.concatenate([st_a_conv[j], proj[:, :, :a3]], axis=1)[:, -(CONV_W - 1):])
            b_conv.append(jnp.concatenate([st_b_conv[j], proj[:, :, xb0:xb0 + B_WIDTH]], axis=1)[:, -(CONV_W - 1):])
            a_delta.append(s_new)
            b_h.append(h_last)
            x2 = matmul_res([o_a.reshape(n, A_WIDTH), o_b.reshape(n, B_WIDTH)],
                            [w['ab_out_a'][j], w['ab_out_b'][j]], x2)
        else:
            lam_init = 0.8 - 0.6 * math.exp(-0.3 * l)
            qkv = norm_matmul(x2, p['norm_mix'][l], w['c_qkv'][j]).reshape(b, t, 3 * C_WIDTH)
            kn = qkv[:, :, C_WIDTH:2 * C_WIDTH]
            vn = qkv[:, :, 2 * C_WIDTH:]
            if c_k is None:
                o_c = diff_attn(qkv, qkv, 0, C_HEADS, 2 * C_HEADS, w['c_lam'][j], p['c_norm'][j], past, lam_init)
            else:
                k_all = jnp.concatenate([c_k[j].reshape(b, -1, C_WIDTH), kn], axis=1)
                v_all = jnp.concatenate([c_v[j].reshape(b, -1, C_WIDTH), vn], axis=1)
                kv = jnp.concatenate([k_all, v_all], axis=2)
                o_c = diff_attn(qkv, kv, 0, 0, C_HEADS, w['c_lam'][j], p['c_norm'][j], past, lam_init)
            new_k.append(kn.reshape(b, t, C_HEADS, 2, C_DH))
            new_v.append(vn.reshape(b, t, C_HEADS, C_VD))
            x2 = matmul_res([o_c.reshape(n, C_WIDTH)], [w['c_out'][j]], x2)
        x2 = _peer(x2, l, p, w, final_norm=(l == depth - 1))
    return (x2.reshape(b, t, d), jnp.stack(a_conv), jnp.stack(a_delta), jnp.stack(b_conv), jnp.stack(b_h),
            jnp.stack(new_k), jnp.stack(new_v))


def kernel(x_prompt, x_sample, state_a_conv, state_a_delta, state_b_conv, state_b_rglru, cache_c_k, cache_c_v, norm_mix, norm_ffn, norm_final, ab_w_in, ab_w_out, a_conv_w, a_log, a_dt_bias, a_norm, b_conv_w, b_conv_b, b_gate_a_w, b_gate_a_b, b_gate_x_w, b_gate_x_b, b_lambda, c_w_qkv, c_w_out, c_lambda_q1, c_lambda_k1, c_lambda_q2, c_lambda_k2, c_norm, peer_w_query, peer_sub_keys, peer_u, peer_v):
    p = dict(norm_mix=norm_mix, norm_ffn=norm_ffn, norm_final=norm_final, ab_w_in=ab_w_in,
             ab_w_out=ab_w_out, a_conv_w=a_conv_w, a_log=a_log, a_dt_bias=a_dt_bias, a_norm=a_norm,
             b_conv_w=b_conv_w, b_conv_b=b_conv_b, b_gate_a_w=b_gate_a_w, b_gate_a_b=b_gate_a_b,
             b_gate_x_w=b_gate_x_w, b_gate_x_b=b_gate_x_b, b_lambda=b_lambda, c_w_qkv=c_w_qkv,
             c_w_out=c_w_out, c_lambda_q1=c_lambda_q1, c_lambda_k1=c_lambda_k1,
             c_lambda_q2=c_lambda_q2, c_lambda_k2=c_lambda_k2, c_norm=c_norm,
             peer_w_query=peer_w_query, peer_sub_keys=peer_sub_keys, peer_u=peer_u, peer_v=peer_v)
    w = _prep_weights(p)
    bp = x_prompt.shape[0]
    dt = x_prompt.dtype
    n_even = state_a_conv.shape[0]
    z_a_conv = jnp.zeros((n_even, bp) + state_a_conv.shape[2:], dt)
    z_a_delta = jnp.zeros((n_even, bp) + state_a_delta.shape[2:], dt)
    z_b_conv = jnp.zeros((n_even, bp) + state_b_conv.shape[2:], dt)
    z_b_h = jnp.zeros((n_even, bp) + state_b_rglru.shape[2:], dt)
    (y_prompt, pa_conv, pa_delta, pb_conv, pb_h, pc_k, pc_v) = _trunk(
        x_prompt, z_a_conv, z_a_delta, z_b_conv, z_b_h, None, None, p, w, 0)
    (y_sample, sa_conv, sa_delta, sb_conv, sb_h, sc_k, sc_v) = _trunk(
        x_sample, state_a_conv, state_a_delta, state_b_conv, state_b_rglru, cache_c_k, cache_c_v,
        p, w, cache_c_k.shape[2])
    return (y_prompt, y_sample, pa_conv, pa_delta, pb_conv, pb_h, pc_k, pc_v,
            sa_conv, sa_delta, sb_conv, sb_h, sc_k, sc_v)
```

```python
import functools
import math

import jax
import jax.numpy as jnp
from jax import lax
from jax.experimental import pallas as pl
from jax.experimental.pallas import tpu as pltpu

F32 = jnp.float32
BF16 = jnp.bfloat16
HI = lax.Precision.HIGHEST

EPS = 1e-6
CHUNK = 64
CONV_W = 4
A_HEADS = 4
A_DK = 128
A_WIDTH = A_HEADS * A_DK
B_WIDTH = 512
B_BLOCKS = 8
RG_C = 8.0
C_HEADS = 8
C_DH = 64
C_VD = 2 * C_DH
C_WIDTH = C_HEADS * C_VD
PEER_HEADS = 8
N_KEYS = 128
PEER_TOPK = 16
PEER_DQ = 256
LANES = 128
VMEM_LIMIT = 48 * 1024 * 1024


def _cparams(sem):
    return pltpu.CompilerParams(dimension_semantics=sem, vmem_limit_bytes=VMEM_LIMIT)


def _dot(a, b, prec=None):
    return jnp.dot(a, b, precision=prec, preferred_element_type=F32)


def _dot_nt(a, b, prec=None):
    return lax.dot_general(a, b, (((1,), (1,)), ((), ())), precision=prec, preferred_element_type=F32)


def _dot_bf16(a, b):
    return _dot(a.astype(BF16), b.astype(BF16))


def _dot_sel(sel, b, nt=False):
    f = _dot_nt if nt else _dot
    s = sel.astype(F32).astype(BF16)
    b1 = b.astype(BF16)
    r1 = b - b1.astype(F32)
    b2 = r1.astype(BF16)
    b3 = (r1 - b2.astype(F32)).astype(BF16)
    return f(s, b1) + f(s, b2) + f(s, b3)


def _rms(x, g):
    return x * lax.rsqrt(jnp.mean(x * x, axis=-1, keepdims=True) + EPS) * g


def _softplus(x):
    return jnp.maximum(x, 0.0) + jnp.log1p(jnp.exp(-jnp.abs(x)))


def _row_tile(n, pref):
    t = min(n, pref)
    while n % t:
        t //= 2
    return t


def _norm_matmul_kernel(x_ref, g_ref, w_ref, *o_refs):
    y = _rms(x_ref[...], g_ref[...])
    res = _dot(y.astype(BF16), w_ref[...])
    col = 0
    for o_ref in o_refs:
        o_ref[...] = res[:, col:col + o_ref.shape[1]]
        col += o_ref.shape[1]


def norm_matmul(x, g, w, n_out=1):
    n, d = x.shape
    m = w.shape[1]
    mo = m // n_out
    assert mo * n_out == m and (n_out == 1 or mo % LANES == 0)
    tm = _row_tile(n, 256)
    outs = pl.pallas_call(
        _norm_matmul_kernel,
        grid=(n // tm,),
        in_specs=[
            pl.BlockSpec((tm, d), lambda i: (i, 0)),
            pl.BlockSpec((1, d), lambda i: (0, 0)),
            pl.BlockSpec((d, m), lambda i: (0, 0)),
        ],
        out_specs=[pl.BlockSpec((tm, mo), lambda i: (i, 0)) for _ in range(n_out)],
        out_shape=[jax.ShapeDtypeStruct((n, mo), F32) for _ in range(n_out)],
        compiler_params=_cparams(("parallel",)),
        name="norm_matmul",
    )(x, g.reshape(1, d), w)
    return outs[0] if n_out == 1 else outs


def _matmul_res_kernel(*refs, n_in):
    ys = refs[:n_in]
    ws = refs[n_in:2 * n_in]
    res_ref = refs[2 * n_in]
    o_ref = refs[2 * n_in + 1]
    acc = res_ref[...]
    for y_ref, w_ref in zip(ys, ws):
        acc = acc + _dot(y_ref[...].astype(BF16), w_ref[...])
    o_ref[...] = acc


def matmul_res(ys, ws, res):
    n, d = res.shape
    tm = _row_tile(n, 512)
    n_in = len(ys)
    in_specs = [pl.BlockSpec((tm, y.shape[1]), lambda i: (i, 0)) for y in ys]
    in_specs += [pl.BlockSpec(w.shape, lambda i: (0, 0)) for w in ws]
    in_specs += [pl.BlockSpec((tm, d), lambda i: (i, 0))]
    return pl.pallas_call(
        functools.partial(_matmul_res_kernel, n_in=n_in),
        grid=(n // tm,),
        in_specs=in_specs,
        out_specs=pl.BlockSpec((tm, d), lambda i: (i, 0)),
        out_shape=jax.ShapeDtypeStruct((n, d), F32),
        compiler_params=_cparams(("parallel",)),
        name="matmul_res",
    )(*ys, *ws, res)


def _gdn_kernel(qkv_ref, z_ref, ba_ref, cprev_ref, s0_ref, cw_ref, prm_ref, an_ref,
                o_ref, sout_ref, xp_ref, s_ref, *, chunk):
    c = chunk
    nh = A_HEADS
    r = nh * c
    step = pl.program_id(1)

    @pl.when(step == 0)
    def _():
        xp_ref[5:8, :] = cprev_ref[0]
        for h in range(nh):
            s_ref[:, h * A_DK:(h + 1) * A_DK] = s0_ref[0, h]

    xp_ref[8:8 + c, :] = qkv_ref[0]
    conv = xp_ref[5:5 + c, :] * cw_ref[0:1, :]
    for j in range(1, CONV_W):
        conv = conv + xp_ref[5 + j:5 + j + c, :] * cw_ref[j:j + 1, :]
    xp_ref[5:8, :] = xp_ref[5 + c:8 + c, :]
    act = conv * jax.nn.sigmoid(conv)

    def stack(off):
        return jnp.concatenate([act[:, off + h * A_DK: off + (h + 1) * A_DK] for h in range(nh)], axis=0)

    qs, ks, vs = stack(0), stack(A_WIDTH), stack(2 * A_WIDTH)
    qs = qs * lax.rsqrt(jnp.sum(qs * qs, axis=-1, keepdims=True) + EPS) * (A_DK ** -0.5)
    ks = ks * lax.rsqrt(jnp.sum(ks * ks, axis=-1, keepdims=True) + EPS)

    ba = ba_ref[0]
    beta_full = jax.nn.sigmoid(ba)
    g_full = -jnp.exp(prm_ref[0:1, :]) * _softplus(ba + prm_ref[1:2, :])
    beta_s = jnp.concatenate([beta_full[:, h:h + 1] for h in range(nh)], axis=0)
    g_s = jnp.concatenate([g_full[:, nh + h:nh + h + 1] for h in range(nh)], axis=0)

    row = lax.broadcasted_iota(jnp.int32, (r, r), 0)
    col = lax.broadcasted_iota(jnp.int32, (r, r), 1)
    same = (row // c) == (col // c)
    m_incl = same & (col <= row)
    m_strict = same & (col < row)

    gcs = _dot_sel(m_incl, jnp.broadcast_to(g_s, (r, LANES)))
    lane0 = lax.broadcasted_iota(jnp.int32, (r, LANES), 1) == 0
    gcs_row = _dot_sel(lane0, gcs, nt=True)
    gc = gcs[:, 0:1]
    decay = jnp.exp(jnp.where(m_incl, gc - gcs_row, -jnp.inf))

    kb = ks * beta_s
    ks_b = ks.astype(BF16)
    a_mat = jnp.where(m_strict, _dot_nt(kb.astype(BF16), ks_b), 0.0) * decay
    eye = (row == col).astype(F32)
    p = -a_mat
    t_inv = eye + p
    for _ in range(int(math.log2(c)) - 1):
        p = _dot_bf16(p, p)
        t_inv = t_inv + _dot_bf16(t_inv, p)
    uw = _dot_bf16(t_inv, jnp.concatenate([vs * beta_s, kb * jnp.exp(gc)], axis=1))
    u, w = uw[:, :A_DK], uw[:, A_DK:]
    qk = jnp.where(m_incl, _dot_nt(qs.astype(BF16), ks_b), 0.0) * decay

    def diag_blocks(x):
        return jnp.concatenate([x[h * c:(h + 1) * c, h * A_DK:(h + 1) * A_DK] for h in range(nh)], axis=0)

    s_cat = s_ref[...]
    wq_s = _dot_bf16(jnp.concatenate([w, qs * jnp.exp(gc)], axis=0), s_cat)
    v_new = u - diag_blocks(wq_s[:r])
    o_s = diag_blocks(wq_s[r:]) + _dot_bf16(qk, v_new)

    g_last = jnp.concatenate(
        [jnp.broadcast_to(gcs[(h + 1) * c - 1:(h + 1) * c, :], (c, LANES)) for h in range(nh)], axis=0)
    kdec = ks * jnp.exp(g_last - gcs)
    zero = jnp.zeros((c, A_DK), F32)
    v_exp = jnp.concatenate(
        [jnp.concatenate([v_new[h * c:(h + 1) * c] if hh == h else zero for hh in range(nh)], axis=1)
         for h in range(nh)], axis=0)
    s_scale = jnp.concatenate(
        [jnp.exp(jnp.broadcast_to(gcs[(h + 1) * c - 1:(h + 1) * c, :], (A_DK, LANES))) for h in range(nh)], axis=1)
    s_new = s_cat * s_scale + _dot_bf16(kdec.T, v_exp)
    s_ref[...] = s_new

    o_n = _rms(o_s, an_ref[...])
    o_full = jnp.concatenate([o_n[h * c:(h + 1) * c] for h in range(nh)], axis=1)
    zz = z_ref[0]
    o_ref[0] = o_full * (zz * jax.nn.sigmoid(zz))

    @pl.when(step == pl.num_programs(1) - 1)
    def _():
        for h in range(nh):
            sout_ref[0, h] = s_new[:, h * A_DK:(h + 1) * A_DK]


def gdn(proj, conv_prev, s0, conv_w, a_log, dt_bias, a_norm, chunk):
    b, t, _ = proj.shape
    nsteps = t // chunk
    prm = jnp.zeros((2, LANES), F32)
    prm = prm.at[0, A_HEADS:2 * A_HEADS].set(a_log).at[1, A_HEADS:2 * A_HEADS].set(dt_bias)
    qkv_w = 3 * A_WIDTH
    return pl.pallas_call(
        functools.partial(_gdn_kernel, chunk=chunk),
        grid=(b, nsteps),
        in_specs=[
            pl.BlockSpec((1, chunk, qkv_w), lambda i, j: (i, j, 0)),
            pl.BlockSpec((1, chunk, A_WIDTH), lambda i, j: (i, j, qkv_w // A_WIDTH)),
            pl.BlockSpec((1, chunk, LANES), lambda i, j: (i, j, (qkv_w + 3 * A_WIDTH) // LANES)),
            pl.BlockSpec((1, CONV_W - 1, qkv_w), lambda i, j: (i, 0, 0)),
            pl.BlockSpec((1, A_HEADS, A_DK, A_DK), lambda i, j: (i, 0, 0, 0)),
            pl.BlockSpec((CONV_W, qkv_w), lambda i, j: (0, 0)),
            pl.BlockSpec((2, LANES), lambda i, j: (0, 0)),
            pl.BlockSpec((1, A_DK), lambda i, j: (0, 0)),
        ],
        out_specs=[
            pl.BlockSpec((1, chunk, A_WIDTH), lambda i, j: (i, j, 0)),
            pl.BlockSpec((1, A_HEADS, A_DK, A_DK), lambda i, j: (i, 0, 0, 0)),
        ],
        out_shape=[
            jax.ShapeDtypeStruct((b, t, A_WIDTH), F32),
            jax.ShapeDtypeStruct((b, A_HEADS, A_DK, A_DK), F32),
        ],
        scratch_shapes=[
            pltpu.VMEM((chunk + 8, qkv_w), F32),
            pltpu.VMEM((A_DK, A_HEADS * A_DK), F32),
        ],
        compiler_params=_cparams(("parallel", "arbitrary")),
        name="gdn",
    )(proj, proj, proj, conv_prev, s0, conv_w, prm, a_norm.reshape(1, A_DK))


def _lru_kernel(xb_ref, gb_ref, cprev_ref, h0_ref, cw_ref, cb_ref, wa_ref, ba_ref, wx_ref, bx_ref, lam_ref,
                ob_ref, hl_ref, xp_ref, a_ref, u_ref, hs_ref, h_ref, *, tb):
    step = pl.program_id(1)

    @pl.when(step == 0)
    def _():
        xp_ref[5:8, :] = cprev_ref[0]
        h_ref[...] = h0_ref[0]

    xp_ref[8:8 + tb, :] = xb_ref[0]
    xc = xp_ref[5:5 + tb, :] * cw_ref[0:1, :]
    for j in range(1, CONV_W):
        xc = xc + xp_ref[5 + j:5 + j + tb, :] * cw_ref[j:j + 1, :]
    xp_ref[5:8, :] = xp_ref[5 + tb:8 + tb, :]
    xc = xc + cb_ref[...]
    xcb = xc.astype(BF16)
    rg = jax.nn.sigmoid(_dot(xcb, wa_ref[...]) + ba_ref[...])
    ig = jax.nn.sigmoid(_dot(xcb, wx_ref[...]) + bx_ref[...])
    log_a = -RG_C * rg * _softplus(-lam_ref[...])
    a_ref[...] = jnp.exp(log_a)
    u_ref[...] = jnp.sqrt(1.0 - jnp.exp(2.0 * log_a)) * (ig * xc)

    def body(i, h):
        h = a_ref[pl.ds(i, 1), :] * h + u_ref[pl.ds(i, 1), :]
        hs_ref[pl.ds(i, 1), :] = h
        return h

    h = lax.fori_loop(0, tb, body, h_ref[...], unroll=8)
    h_ref[...] = h
    ob_ref[0] = hs_ref[...] * jax.nn.gelu(gb_ref[0])

    @pl.when(step == pl.num_programs(1) - 1)
    def _():
        hl_ref[0] = h


def lru(proj, conv_prev, h0, conv_w, conv_b, wa_bd, ba, wx_bd, bx, lam):
    b, t, _ = proj.shape
    tb = _row_tile(t, 256)
    xb_blk = (3 * A_WIDTH + A_WIDTH) // B_WIDTH
    vec = lambda v: v.reshape(1, B_WIDTH)
    const = lambda shape: pl.BlockSpec(shape, lambda i, j: (0,) * len(shape))
    ob, hl = pl.pallas_call(
        functools.partial(_lru_kernel, tb=tb),
        grid=(b, t // tb),
        in_specs=[
            pl.BlockSpec((1, tb, B_WIDTH), lambda i, j: (i, j, xb_blk)),
            pl.BlockSpec((1, tb, B_WIDTH), lambda i, j: (i, j, xb_blk + 1)),
            pl.BlockSpec((1, CONV_W - 1, B_WIDTH), lambda i, j: (i, 0, 0)),
            pl.BlockSpec((1, 1, B_WIDTH), lambda i, j: (i, 0, 0)),
            const((CONV_W, B_WIDTH)), const((1, B_WIDTH)),
            const((B_WIDTH, B_WIDTH)), const((1, B_WIDTH)),
            const((B_WIDTH, B_WIDTH)), const((1, B_WIDTH)),
            const((1, B_WIDTH)),
        ],
        out_specs=[
            pl.BlockSpec((1, tb, B_WIDTH), lambda i, j: (i, j, 0)),
            pl.BlockSpec((1, 1, B_WIDTH), lambda i, j: (i, 0, 0)),
        ],
        out_shape=[
            jax.ShapeDtypeStruct((b, t, B_WIDTH), F32),
            jax.ShapeDtypeStruct((b, 1, B_WIDTH), F32),
        ],
        scratch_shapes=[
            pltpu.VMEM((tb + 8, B_WIDTH), F32),
            pltpu.VMEM((tb, B_WIDTH), F32),
            pltpu.VMEM((tb, B_WIDTH), F32),
            pltpu.VMEM((tb, B_WIDTH), F32),
            pltpu.VMEM((1, B_WIDTH), F32),
        ],
        compiler_params=_cparams(("parallel", "arbitrary")),
        name="lru",
    )(proj, proj, conv_prev, h0.reshape(b, 1, B_WIDTH), conv_w, vec(conv_b), wa_bd, vec(ba), wx_bd, vec(bx), vec(lam))
    return ob, hl.reshape(b, B_WIDTH)


def _attn_kernel(q_ref, kp_ref, vp_ref, kn_ref, vn_ref, lam_ref, ng_ref, o_ref, *, tq, nq, past, lam_init):
    qi = pl.program_id(2)
    q = q_ref[0] * (C_DH ** -0.5)
    lane = lax.broadcasted_iota(jnp.int32, q.shape, 1)
    qc = (jnp.where(lane < C_DH, q, 0.0).astype(BF16), jnp.where(lane >= C_DH, q, 0.0).astype(BF16))
    lv = lam_ref[...]
    lam = (jnp.exp(jnp.sum(lv[0:1] * lv[1:2], axis=-1, keepdims=True))
           - jnp.exp(jnp.sum(lv[2:3] * lv[3:4], axis=-1, keepdims=True)) + lam_init)
    rpos = lax.broadcasted_iota(jnp.int32, (tq, tq), 0)
    cpos = lax.broadcasted_iota(jnp.int32, (tq, tq), 1)
    tail_mask = (cpos // CHUNK) <= (rpos // CHUNK)

    def tile(i):
        p0 = past + i * tq
        kt = kn_ref[0, i * tq:(i + 1) * tq, :].astype(BF16)
        vt = vn_ref[0, i * tq:(i + 1) * tq, :].astype(BF16)
        if p0:
            kf = kp_ref[0, 0:p0, :].astype(BF16)
            vf = vp_ref[0, 0:p0, :].astype(BF16)
        o = None
        for comp in range(2):
            st = jnp.where(tail_mask, _dot_nt(qc[comp], kt), -jnp.inf)
            m = jnp.max(st, axis=-1, keepdims=True)
            if p0:
                sf = _dot_nt(qc[comp], kf)
                m = jnp.maximum(m, jnp.max(sf, axis=-1, keepdims=True))
            et = jnp.exp(st - m)
            l = jnp.sum(et, axis=-1, keepdims=True)
            acc = _dot(et.astype(BF16), vt)
            if p0:
                ef = jnp.exp(sf - m)
                l = l + jnp.sum(ef, axis=-1, keepdims=True)
                acc = acc + _dot(ef.astype(BF16), vf)
            oc = acc / l
            o = oc if comp == 0 else o - lam * oc
        o_ref[0] = _rms(o, ng_ref[...]) * (1.0 - lam_init)

    for i in range(nq):
        pl.when(qi == i)(functools.partial(tile, i))


def diff_attn(q, k_new, v_new, k_past, v_past, lam_vecs, norm_g, lam_init):
    b, t_q, _ = q.shape
    past = 0 if k_past is None else k_past.shape[1]
    tq = _row_tile(t_q, 256)
    nq = t_q // tq
    assert past % CHUNK == 0 and (nq == 1 or (tq % CHUNK == 0 and past == 0))
    k_pre, v_pre = (k_new, v_new) if k_past is None else (k_past, v_past)
    t_pre = k_pre.shape[1]
    head_blk = lambda rows: pl.BlockSpec((1, rows, C_VD), lambda i, h, j: (i, 0, h))
    return pl.pallas_call(
        functools.partial(_attn_kernel, tq=tq, nq=nq, past=past, lam_init=lam_init),
        grid=(b, C_HEADS, nq),
        in_specs=[
            pl.BlockSpec((1, tq, C_VD), lambda i, h, j: (i, j, h)),
            head_blk(t_pre), head_blk(t_pre), head_blk(t_q), head_blk(t_q),
            pl.BlockSpec((4, C_DH), lambda i, h, j: (0, 0)),
            pl.BlockSpec((1, C_VD), lambda i, h, j: (0, 0)),
        ],
        out_specs=pl.BlockSpec((1, tq, C_VD), lambda i, h, j: (i, j, h)),
        out_shape=jax.ShapeDtypeStruct((b, t_q, C_WIDTH), F32),
        compiler_params=_cparams(("parallel", "parallel", "arbitrary")),
        name="diff_attn",
    )(q, k_pre, v_pre, k_new, v_new, lam_vecs, norm_g.reshape(1, C_VD))


def _topk_extract(vals, k, payload=None):
    n_rows = vals.shape[0]
    rio = lax.broadcasted_iota(jnp.int32, vals.shape, 0)
    out_v, out_p = [], []
    for _ in range(k):
        m = jnp.max(vals, axis=0, keepdims=True)
        am = jnp.min(jnp.where(vals == m, rio, n_rows), axis=0, keepdims=True)
        hit = rio == am
        out_v.append(m)
        out_p.append(am if payload is None else jnp.max(jnp.where(hit, payload, -1), axis=0, keepdims=True))
        vals = jnp.where(hit, -jnp.inf, vals)
    return jnp.concatenate(out_v, axis=0), jnp.concatenate(out_p, axis=0), vals


def _topk_rows(vals, k, payload=None):
    return _topk_extract(vals, k, payload)[:2]


_CAND_PER_ROW = [PEER_TOPK // (i + 1) for i in range(PEER_TOPK)]
_N_CAND = sum(_CAND_PER_ROW)
_N_CAND_PAD = -(-_N_CAND // 8) * 8


TOPK_TOKENS = 256


def _subkey_scores(q_ref, keys_ref, h, p):
    half = PEER_DQ // 2
    off = pl.multiple_of(h * PEER_DQ + p * half, half)
    return _dot_nt(keys_ref[p], q_ref[:, pl.ds(off, half)], HI)


def _subkey_top(q_ref, keys_ref, h, p):
    return _topk_rows(_subkey_scores(q_ref, keys_ref, h, p), PEER_TOPK)


def _pair_top(s1, i1, s2, i2, h, cs_ref, ce_ref, idx_t, gate_t):
    tt = s1.shape[1]
    cs_ref[_N_CAND:, :] = jnp.full((_N_CAND_PAD - _N_CAND, tt), -jnp.inf, F32)
    ce_ref[_N_CAND:, :] = jnp.full((_N_CAND_PAD - _N_CAND, tt), -1, jnp.int32)
    row0 = 0
    for i, cnt in enumerate(_CAND_PER_ROW):
        cs_ref[row0:row0 + cnt, :] = s1[i:i + 1] + s2[0:cnt]
        ce_ref[row0:row0 + cnt, :] = i1[i:i + 1] * N_KEYS + i2[0:cnt]
        row0 += cnt
    top_s, top_e = _topk_rows(cs_ref[...], PEER_TOPK, ce_ref[...])
    e = jnp.exp(top_s - top_s[0:1])
    gate = e / jnp.sum(e, axis=0, keepdims=True)
    r0 = pl.multiple_of(h * PEER_TOPK, PEER_TOPK)
    idx_t[pl.ds(r0, PEER_TOPK), :] = top_e
    gate_t[pl.ds(r0, PEER_TOPK), :] = gate


def _topk_scratch(tt):
    return [pltpu.VMEM((NSEL, tt), jnp.int32), pltpu.VMEM((NSEL, tt), F32),
            pltpu.VMEM((_N_CAND_PAD, tt), F32), pltpu.VMEM((_N_CAND_PAD, tt), jnp.int32)]


def _peer_topk_kernel(q_ref, keys_ref, idx_ref, gate_ref, idx_t, gate_t, cs_ref, ce_ref):
    def head(h, carry):
        s1, i1 = _subkey_top(q_ref, keys_ref, h, 0)
        s2, i2 = _subkey_top(q_ref, keys_ref, h, 1)
        _pair_top(s1, i1, s2, i2, h, cs_ref, ce_ref, idx_t, gate_t)
        return carry

    lax.fori_loop(0, PEER_HEADS, head, 0)
    idx_ref[...] = idx_t[...].T
    gate_ref[...] = gate_t[...].T


def peer_topk(qry, sub_keys, n_rows):
    n = n_rows
    tt = _row_tile(n, TOPK_TOKENS)
    nsel = NSEL
    return pl.pallas_call(
        _peer_topk_kernel,
        grid=(n // tt,),
        in_specs=[
            pl.BlockSpec((tt, PEER_HEADS * PEER_DQ), lambda i: (i, 0)),
            pl.BlockSpec((2, N_KEYS, PEER_DQ // 2), lambda i: (0, 0, 0)),
        ],
        out_specs=[
            pl.BlockSpec((tt, nsel), lambda i: (i, 0)),
            pl.BlockSpec((tt, nsel), lambda i: (i, 0)),
        ],
        out_shape=[
            jax.ShapeDtypeStruct((n, nsel), jnp.int32),
            jax.ShapeDtypeStruct((n, nsel), F32),
        ],
        scratch_shapes=_topk_scratch(tt),
        compiler_params=_cparams(("parallel",)),
        name="peer_topk",
    )(qry, sub_keys)


PEER_GROUP = 8
NSEL = PEER_HEADS * PEER_TOPK
SLAB_ROWS = 8
SLAB_PITCH = 12


def peer_table(u, v):
    ne, d = u.shape
    assert d == SLAB_ROWS * LANES
    hi = lax.bitcast_convert_type(u.astype(BF16), jnp.uint16).astype(jnp.uint32)
    lo = lax.bitcast_convert_type(v.astype(BF16), jnp.uint16).astype(jnp.uint32)
    return ((hi << 16) | lo).reshape(ne, SLAB_ROWS, LANES)


SIDE_SLICES = 2 * PEER_HEADS


def _peer_expert_kernel(*refs, final_norm, side):
    if side:
        (idx_ref, idxn_ref, x_ref, gate_ref, g_ref, gf_ref, uv_hbm, qn_ref, keys_ref,
         o_ref, idxo_ref, gateo_ref, gbuf0, gbuf1, sem, idx_t, gate_t, cs_ref, ce_ref, s_keep, i_keep,
         s_half, i_half, rest_ref) = refs
    else:
        idx_ref, idxn_ref, x_ref, gate_ref, g_ref, gf_ref, uv_hbm, o_ref, gbuf0, gbuf1, sem = refs
    g = PEER_GROUP
    d = x_ref.shape[1]
    nsub = d // LANES
    step = pl.program_id(0)
    gbufs = (gbuf0, gbuf1)

    half_k = PEER_TOPK // 2

    def side_first_half():
        k = step % SIDE_SLICES
        sc = _subkey_scores(qn_ref, keys_ref, k // 2, k % 2)
        s, i, rest = _topk_extract(sc, half_k)
        s_half[...] = s
        i_half[...] = i
        rest_ref[...] = rest

    def side_second_half():
        k = step % SIDE_SLICES
        h = k // 2
        p = k % 2
        s, i, _ = _topk_extract(rest_ref[...], half_k)
        s = jnp.concatenate([s_half[...], s], axis=0)
        i = jnp.concatenate([i_half[...], i], axis=0)

        @pl.when(p == 0)
        def _():
            s_keep[...] = s
            i_keep[...] = i

        @pl.when(p == 1)
        def _():
            _pair_top(s_keep[...], i_keep[...], s, i, h, cs_ref, ce_ref, idx_t, gate_t)

        @pl.when(k == SIDE_SLICES - 1)
        def _():
            idxo_ref[...] = idx_t[...].T
            gateo_ref[...] = gate_t[...].T

    def slab_copy(e, slot, t, r):
        row0 = (t * NSEL + r) * SLAB_PITCH
        return pltpu.make_async_copy(uv_hbm.at[e], gbufs[slot].at[pl.ds(row0, SLAB_ROWS), :], sem.at[slot])

    def issue_token(iref, t_src, slot, t):
        for r in range(NSEL):
            slab_copy(iref[t_src, r], slot, t, r).start(priority=r % 2)

    def wait_group(slot):
        def body(t, c):
            for r in range(NSEL):
                slab_copy(0, slot, 0, 0).wait()
            return c
        lax.fori_loop(0, g, body, 0)

    def expert_rows(slot, t):
        base = t * NSEL * SLAB_PITCH
        words = jnp.concatenate(
            [gbufs[slot][pl.ds(base + s, NSEL, stride=SLAB_PITCH), :] for s in range(nsub)], axis=1)
        u = lax.bitcast_convert_type(words & jnp.uint32(0xFFFF0000), F32).astype(BF16)
        v = lax.bitcast_convert_type(words << 16, F32).astype(BF16)
        return u, v

    def combine_group(row0, slot, iref, t_src0, slot_next):
        x = x_ref[row0:row0 + g, :]
        xnb = _rms(x, g_ref[...]).astype(BF16)
        gate = gate_ref[row0:row0 + g, :]
        rio = lax.broadcasted_iota(jnp.int32, (g, NSEL), 0)
        out = x
        for t in range(g):
            issue_token(iref, t_src0 + t, slot_next, t)
            u, v = expert_rows(slot, t)
            act = _dot_nt(xnb, u)
            w = jnp.where(rio == t, gate * jax.nn.gelu(act), 0.0).astype(BF16)
            out = out + _dot(w, v)
        if final_norm:
            out = _rms(out, gf_ref[...])
        o_ref[row0:row0 + g, :] = out

    @pl.when(step == 0)
    def _():
        for t in range(g):
            issue_token(idx_ref, t, 0, t)

    if side:
        side_first_half()
    wait_group(0)
    combine_group(0, 0, idx_ref, g, 1)
    if side:
        side_second_half()
    wait_group(1)
    combine_group(g, 1, idxn_ref, 0, 0)

    @pl.when(step == pl.num_programs(0) - 1)
    def _():
        wait_group(0)


def peer_experts(x, idx, gate, norm_g, final_g, uv, final_norm, row0=0, side=None):
    n = idx.shape[0]
    d = x.shape[1]
    g = PEER_GROUP
    step_rows = 2 * g
    nsteps = n // step_rows
    ngroups = 2 * nsteps
    assert n % step_rows == 0 and row0 % step_rows == 0
    blk0 = row0 // step_rows
    in_specs = [
        pl.BlockSpec((step_rows, NSEL), lambda i: (i, 0), memory_space=pltpu.SMEM),
        pl.BlockSpec((g, NSEL), lambda i: (jnp.minimum(2 * i + 2, ngroups - 1), 0), memory_space=pltpu.SMEM),
        pl.BlockSpec((step_rows, d), lambda i: (i + blk0, 0)),
        pl.BlockSpec((step_rows, NSEL), lambda i: (i, 0)),
        pl.BlockSpec((1, d), lambda i: (0, 0)),
        pl.BlockSpec((1, d), lambda i: (0, 0)),
        pl.BlockSpec(memory_space=pl.ANY),
    ]
    args = [idx, idx, x, gate, norm_g.reshape(1, d), final_g.reshape(1, d), uv]
    out_specs = [pl.BlockSpec((step_rows, d), lambda i: (i, 0))]
    out_shape = [jax.ShapeDtypeStruct((n, d), F32)]
    scratch = [
        pltpu.VMEM((g * NSEL * SLAB_PITCH, LANES), jnp.uint32),
        pltpu.VMEM((g * NSEL * SLAB_PITCH, LANES), jnp.uint32),
        pltpu.SemaphoreType.DMA((2,)),
    ]
    if side is not None:
        qry, sub_keys, side_row0 = side
        tt = TOPK_TOKENS
        assert tt == SIDE_SLICES * step_rows and n % tt == 0 and side_row0 % tt == 0
        sblk0 = side_row0 // tt
        in_specs += [
            pl.BlockSpec((tt, qry.shape[1]), lambda i: (i // SIDE_SLICES + sblk0, 0)),
            pl.BlockSpec(sub_keys.shape, lambda i: (0, 0, 0)),
        ]
        args += [qry, sub_keys]
        out_specs += [pl.BlockSpec((tt, NSEL), lambda i: (i // SIDE_SLICES, 0)),
                      pl.BlockSpec((tt, NSEL), lambda i: (i // SIDE_SLICES, 0))]
        out_shape += [jax.ShapeDtypeStruct((n, NSEL), jnp.int32), jax.ShapeDtypeStruct((n, NSEL), F32)]
        scratch += _topk_scratch(tt) + [
            pltpu.VMEM((PEER_TOPK, tt), F32), pltpu.VMEM((PEER_TOPK, tt), jnp.int32),
            pltpu.VMEM((PEER_TOPK // 2, tt), F32), pltpu.VMEM((PEER_TOPK // 2, tt), jnp.int32),
            pltpu.VMEM((N_KEYS, tt), F32)]
    res = pl.pallas_call(
        functools.partial(_peer_expert_kernel, final_norm=final_norm, side=side is not None),
        grid=(nsteps,),
        in_specs=in_specs,
        out_specs=out_specs,
        out_shape=out_shape,
        scratch_shapes=scratch,
        compiler_params=_cparams(("arbitrary",)),
        name="peer_experts",
    )(*args)
    return res if side is not None else res[0]


def _block_diag(w):
    nb, bi, bj = w.shape
    out = jnp.zeros((nb * bi, nb * bj), w.dtype)
    for i in range(nb):
        out = out.at[i * bi:(i + 1) * bi, i * bj:(i + 1) * bj].set(w[i])
    return out


def _prep_weights(p):
    depth = p['norm_mix'].shape[0]
    d = p['norm_final'].shape[0]
    w = {}
    a3, a4 = 3 * A_WIDTH, 4 * A_WIDTH
    w['ab_in'], w['ab_out_a'], w['ab_out_b'], w['ga'], w['gx'] = [], [], [], [], []
    for j in range(p['ab_w_in'].shape[0]):
        wi = p['ab_w_in'][j]
        pad = jnp.zeros((d, LANES - 2 * A_HEADS), wi.dtype)
        cols = [wi[:, :a4], wi[:, a4 + 2 * A_HEADS:], wi[:, a4:a4 + 2 * A_HEADS], pad]
        w['ab_in'].append(jnp.concatenate(cols, axis=1).astype(BF16))
        w['ab_out_a'].append(p['ab_w_out'][j][:A_WIDTH].astype(BF16))
        w['ab_out_b'].append(p['ab_w_out'][j][A_WIDTH:].astype(BF16))
        w['ga'].append(_block_diag(p['b_gate_a_w'][j]).astype(BF16))
        w['gx'].append(_block_diag(p['b_gate_x_w'][j]).astype(BF16))
    w['c_qkv'] = [m.astype(BF16) for m in p['c_w_qkv']]
    w['c_out'] = [m.astype(BF16) for m in p['c_w_out']]
    w['c_lam'] = [jnp.stack([p['c_lambda_q1'][j], p['c_lambda_k1'][j], p['c_lambda_q2'][j], p['c_lambda_k2'][j]])
                  for j in range(p['c_w_qkv'].shape[0])]
    w['peer_q'] = [p['peer_w_query'][l].astype(BF16) for l in range(depth)]
    w['peer_uv'] = [peer_table(p['peer_u'][l], p['peer_v'][l]) for l in range(depth)]
    return w


PEER_PARTS = 4


def _peer(x2, l, p, w, final_norm):
    n = x2.shape[0]
    keys = p['peer_sub_keys'][l]
    qry = norm_matmul(x2, p['norm_ffn'][l], w['peer_q'][l])
    parts = PEER_PARTS if n % (PEER_PARTS * TOPK_TOKENS) == 0 else 1
    n_part = n // parts
    idx, gate = peer_topk(qry, keys, n_part)
    outs = []
    for i in range(parts):
        side = (qry, keys, (i + 1) * n_part) if i + 1 < parts else None
        res = peer_experts(x2, idx, gate, p['norm_ffn'][l], p['norm_final'], w['peer_uv'][l], final_norm,
                           row0=i * n_part, side=side)
        if side is None:
            outs.append(res)
        else:
            out, idx, gate = res
            outs.append(out)
    return outs[0] if parts == 1 else jnp.concatenate(outs, axis=0)


def _trunk(x, st_a_conv, st_a_delta, st_b_conv, st_b_h, c_k, c_v, p, w, past):
    b, t, d = x.shape
    n = b * t
    chunk = min(CHUNK, t)
    depth = p['norm_mix'].shape[0]
    x2 = x.reshape(n, d)
    a_conv, a_delta, b_conv, b_h, new_k, new_v = [], [], [], [], [], []
    for l in range(depth):
        j = l // 2
        if l % 2 == 0:
            proj = norm_matmul(x2, p['norm_mix'][l], w['ab_in'][j]).reshape(b, t, -1)
            o_a, s_new = gdn(proj, st_a_conv[j], st_a_delta[j], p['a_conv_w'][j], p['a_log'][j],
                             p['a_dt_bias'][j], p['a_norm'][j], chunk)
            o_b, h_last = lru(proj, st_b_conv[j], st_b_h[j], p['b_conv_w'][j], p['b_conv_b'][j], w['ga'][j],
                              p['b_gate_a_b'][j], w['gx'][j], p['b_gate_x_b'][j], p['b_lambda'][j])
            a3 = 3 * A_WIDTH
            xb0 = a3 + A_WIDTH
            a_conv.append(jnp.concatenate([st_a_conv[j], proj[:, :, :a3]], axis=1)[:, -(CONV_W - 1):])
            b_conv.append(jnp.concatenate([st_b_conv[j], proj[:, :, xb0:xb0 + B_WIDTH]], axis=1)[:, -(CONV_W - 1):])
            a_delta.append(s_new)
            b_h.append(h_last)
            x2 = matmul_res([o_a.reshape(n, A_WIDTH), o_b.reshape(n, B_WIDTH)],
                            [w['ab_out_a'][j], w['ab_out_b'][j]], x2)
        else:
            lam_init = 0.8 - 0.6 * math.exp(-0.3 * l)
            qn, kn, vn = (a.reshape(b, t, C_WIDTH) for a in norm_matmul(x2, p['norm_mix'][l], w['c_qkv'][j], n_out=3))
            kp = None if c_k is None else c_k[j].reshape(b, -1, C_WIDTH)
            vp = None if c_v is None else c_v[j].reshape(b, -1, C_WIDTH)
            assert past == (0 if kp is None else kp.shape[1])
            o_c = diff_attn(qn, kn, vn, kp, vp, w['c_lam'][j], p['c_norm'][j], lam_init)
            new_k.append(kn.reshape(b, t, C_HEADS, 2, C_DH))
            new_v.append(vn.reshape(b, t, C_HEADS, C_VD))
            x2 = matmul_res([o_c.reshape(n, C_WIDTH)], [w['c_out'][j]], x2)
        x2 = _peer(x2, l, p, w, final_norm=(l == depth - 1))
    return (x2.reshape(b, t, d), jnp.stack(a_conv), jnp.stack(a_delta), jnp.stack(b_conv), jnp.stack(b_h),
            jnp.stack(new_k), jnp.stack(new_v))


def kernel(x_prompt, x_sample, state_a_conv, state_a_delta, state_b_conv, state_b_rglru, cache_c_k, cache_c_v, norm_mix, norm_ffn, norm_final, ab_w_in, ab_w_out, a_conv_w, a_log, a_dt_bias, a_norm, b_conv_w, b_conv_b, b_gate_a_w, b_gate_a_b, b_gate_x_w, b_gate_x_b, b_lambda, c_w_qkv, c_w_out, c_lambda_q1, c_lambda_k1, c_lambda_q2, c_lambda_k2, c_norm, peer_w_query, peer_sub_keys, peer_u, peer_v):
    p = dict(norm_mix=norm_mix, norm_ffn=norm_ffn, norm_final=norm_final, ab_w_in=ab_w_in,
             ab_w_out=ab_w_out, a_conv_w=a_conv_w, a_log=a_log, a_dt_bias=a_dt_bias, a_norm=a_norm,
             b_conv_w=b_conv_w, b_conv_b=b_conv_b, b_gate_a_w=b_gate_a_w, b_gate_a_b=b_gate_a_b,
             b_gate_x_w=b_gate_x_w, b_gate_x_b=b_gate_x_b, b_lambda=b_lambda, c_w_qkv=c_w_qkv,
             c_w_out=c_w_out, c_lambda_q1=c_lambda_q1, c_lambda_k1=c_lambda_k1,
             c_lambda_q2=c_lambda_q2, c_lambda_k2=c_lambda_k2, c_norm=c_norm,
             peer_w_query=peer_w_query, peer_sub_keys=peer_sub_keys, peer_u=peer_u, peer_v=peer_v)
    w = _prep_weights(p)
    bp = x_prompt.shape[0]
    dt = x_prompt.dtype
    n_even = state_a_conv.shape[0]
    z_a_conv = jnp.zeros((n_even, bp) + state_a_conv.shape[2:], dt)
    z_a_delta = jnp.zeros((n_even, bp) + state_a_delta.shape[2:], dt)
    z_b_conv = jnp.zeros((n_even, bp) + state_b_conv.shape[2:], dt)
    z_b_h = jnp.zeros((n_even, bp) + state_b_rglru.shape[2:], dt)
    (y_prompt, pa_conv, pa_delta, pb_conv, pb_h, pc_k, pc_v) = _trunk(
        x_prompt, z_a_conv, z_a_delta, z_b_conv, z_b_h, None, None, p, w, 0)
    (y_sample, sa_conv, sa_delta, sb_conv, sb_h, sc_k, sc_v) = _trunk(
        x_sample, state_a_conv, state_a_delta, state_b_conv, state_b_rglru, cache_c_k, cache_c_v,
        p, w, cache_c_k.shape[2])
    return (y_prompt, y_sample, pa_conv, pa_delta, pb_conv, pb_h, pc_k, pc_v,
            sa_conv, sa_delta, sb_conv, sb_h, sc_k, sc_v)
```

```python
import functools
import math

import jax
import jax.numpy as jnp
from jax import lax
from jax.experimental import pallas as pl
from jax.experimental.pallas import tpu as pltpu

F32 = jnp.float32
BF16 = jnp.bfloat16
HI = lax.Precision.HIGHEST

EPS = 1e-6
CHUNK = 64
CONV_W = 4
A_HEADS = 4
A_DK = 128
A_WIDTH = A_HEADS * A_DK
B_WIDTH = 512
B_BLOCKS = 8
RG_C = 8.0
C_HEADS = 8
C_DH = 64
C_VD = 2 * C_DH
C_WIDTH = C_HEADS * C_VD
PEER_HEADS = 8
N_KEYS = 128
PEER_TOPK = 16
PEER_DQ = 256
LANES = 128
VMEM_LIMIT = 48 * 1024 * 1024


def _cparams(sem):
    return pltpu.CompilerParams(dimension_semantics=sem, vmem_limit_bytes=VMEM_LIMIT)


def _dot(a, b, prec=None):
    return jnp.dot(a, b, precision=prec, preferred_element_type=F32)


def _dot_nt(a, b, prec=None):
    return lax.dot_general(a, b, (((1,), (1,)), ((), ())), precision=prec, preferred_element_type=F32)


def _dot_bf16(a, b):
    return _dot(a.astype(BF16), b.astype(BF16))


def _dot_sel(sel, b, nt=False):
    f = _dot_nt if nt else _dot
    s = sel.astype(F32).astype(BF16)
    b1 = b.astype(BF16)
    r1 = b - b1.astype(F32)
    b2 = r1.astype(BF16)
    b3 = (r1 - b2.astype(F32)).astype(BF16)
    return f(s, b1) + f(s, b2) + f(s, b3)


def _rms(x, g):
    return x * lax.rsqrt(jnp.mean(x * x, axis=-1, keepdims=True) + EPS) * g


def _softplus(x):
    return jnp.maximum(x, 0.0) + jnp.log1p(jnp.exp(-jnp.abs(x)))


def _row_tile(n, pref):
    t = min(n, pref)
    while n % t:
        t //= 2
    return t


def _norm_matmul_kernel(x_ref, g_ref, w_ref, o_ref):
    y = _rms(x_ref[...], g_ref[...])
    o_ref[...] = _dot(y.astype(BF16), w_ref[...])


def norm_matmul(x, g, w):
    n, d = x.shape
    m = w.shape[1]
    tm = _row_tile(n, 256)
    return pl.pallas_call(
        _norm_matmul_kernel,
        grid=(n // tm,),
        in_specs=[
            pl.BlockSpec((tm, d), lambda i: (i, 0)),
            pl.BlockSpec((1, d), lambda i: (0, 0)),
            pl.BlockSpec((d, m), lambda i: (0, 0)),
        ],
        out_specs=pl.BlockSpec((tm, m), lambda i: (i, 0)),
        out_shape=jax.ShapeDtypeStruct((n, m), F32),
        compiler_params=_cparams(("parallel",)),
        name="norm_matmul",
    )(x, g.reshape(1, d), w)


def _matmul_res_kernel(*refs, n_in):
    ys = refs[:n_in]
    ws = refs[n_in:2 * n_in]
    res_ref = refs[2 * n_in]
    o_ref = refs[2 * n_in + 1]
    acc = res_ref[...]
    for y_ref, w_ref in zip(ys, ws):
        acc = acc + _dot(y_ref[...].astype(BF16), w_ref[...])
    o_ref[...] = acc


def matmul_res(ys, ws, res):
    n, d = res.shape
    tm = _row_tile(n, 512)
    n_in = len(ys)
    in_specs = [pl.BlockSpec((tm, y.shape[1]), lambda i: (i, 0)) for y in ys]
    in_specs += [pl.BlockSpec(w.shape, lambda i: (0, 0)) for w in ws]
    in_specs += [pl.BlockSpec((tm, d), lambda i: (i, 0))]
    return pl.pallas_call(
        functools.partial(_matmul_res_kernel, n_in=n_in),
        grid=(n // tm,),
        in_specs=in_specs,
        out_specs=pl.BlockSpec((tm, d), lambda i: (i, 0)),
        out_shape=jax.ShapeDtypeStruct((n, d), F32),
        compiler_params=_cparams(("parallel",)),
        name="matmul_res",
    )(*ys, *ws, res)


def _gdn_kernel(qkv_ref, z_ref, ba_ref, cprev_ref, s0_ref, cw_ref, prm_ref, an_ref,
                o_ref, sout_ref, xp_ref, s_ref, *, chunk):
    c = chunk
    nh = A_HEADS
    r = nh * c
    step = pl.program_id(1)

    @pl.when(step == 0)
    def _():
        xp_ref[5:8, :] = cprev_ref[0]
        for h in range(nh):
            s_ref[:, h * A_DK:(h + 1) * A_DK] = s0_ref[0, h]

    xp_ref[8:8 + c, :] = qkv_ref[0]
    conv = xp_ref[5:5 + c, :] * cw_ref[0:1, :]
    for j in range(1, CONV_W):
        conv = conv + xp_ref[5 + j:5 + j + c, :] * cw_ref[j:j + 1, :]
    xp_ref[5:8, :] = xp_ref[5 + c:8 + c, :]
    act = conv * jax.nn.sigmoid(conv)

    def stack(off):
        return jnp.concatenate([act[:, off + h * A_DK: off + (h + 1) * A_DK] for h in range(nh)], axis=0)

    qs, ks, vs = stack(0), stack(A_WIDTH), stack(2 * A_WIDTH)
    qs = qs * lax.rsqrt(jnp.sum(qs * qs, axis=-1, keepdims=True) + EPS) * (A_DK ** -0.5)
    ks = ks * lax.rsqrt(jnp.sum(ks * ks, axis=-1, keepdims=True) + EPS)

    ba = ba_ref[0]
    beta_full = jax.nn.sigmoid(ba)
    g_full = -jnp.exp(prm_ref[0:1, :]) * _softplus(ba + prm_ref[1:2, :])
    beta_s = jnp.concatenate([beta_full[:, h:h + 1] for h in range(nh)], axis=0)
    g_s = jnp.concatenate([g_full[:, nh + h:nh + h + 1] for h in range(nh)], axis=0)

    row = lax.broadcasted_iota(jnp.int32, (r, r), 0)
    col = lax.broadcasted_iota(jnp.int32, (r, r), 1)
    same = (row // c) == (col // c)
    m_incl = same & (col <= row)
    m_strict = same & (col < row)

    gcs = _dot_sel(m_incl, jnp.broadcast_to(g_s, (r, LANES)))
    lane0 = lax.broadcasted_iota(jnp.int32, (r, LANES), 1) == 0
    gcs_row = _dot_sel(lane0, gcs, nt=True)
    gc = gcs[:, 0:1]
    decay = jnp.exp(jnp.where(m_incl, gc - gcs_row, -jnp.inf))

    kb = ks * beta_s
    ks_b = ks.astype(BF16)
    a_mat = jnp.where(m_strict, _dot_nt(kb.astype(BF16), ks_b), 0.0) * decay
    eye = (row == col).astype(F32)
    p = -a_mat
    t_inv = eye + p
    for _ in range(int(math.log2(c)) - 1):
        p = _dot_bf16(p, p)
        t_inv = t_inv + _dot_bf16(t_inv, p)
    uw = _dot_bf16(t_inv, jnp.concatenate([vs * beta_s, kb * jnp.exp(gc)], axis=1))
    u, w = uw[:, :A_DK], uw[:, A_DK:]
    qk = jnp.where(m_incl, _dot_nt(qs.astype(BF16), ks_b), 0.0) * decay

    def diag_blocks(x):
        return jnp.concatenate([x[h * c:(h + 1) * c, h * A_DK:(h + 1) * A_DK] for h in range(nh)], axis=0)

    s_cat = s_ref[...]
    wq_s = _dot_bf16(jnp.concatenate([w, qs * jnp.exp(gc)], axis=0), s_cat)
    v_new = u - diag_blocks(wq_s[:r])
    o_s = diag_blocks(wq_s[r:]) + _dot_bf16(qk, v_new)

    g_last = jnp.concatenate(
        [jnp.broadcast_to(gcs[(h + 1) * c - 1:(h + 1) * c, :], (c, LANES)) for h in range(nh)], axis=0)
    kdec = ks * jnp.exp(g_last - gcs)
    zero = jnp.zeros((c, A_DK), F32)
    v_exp = jnp.concatenate(
        [jnp.concatenate([v_new[h * c:(h + 1) * c] if hh == h else zero for hh in range(nh)], axis=1)
         for h in range(nh)], axis=0)
    s_scale = jnp.concatenate(
        [jnp.exp(jnp.broadcast_to(gcs[(h + 1) * c - 1:(h + 1) * c, :], (A_DK, LANES))) for h in range(nh)], axis=1)
    s_new = s_cat * s_scale + _dot_bf16(kdec.T, v_exp)
    s_ref[...] = s_new

    o_n = _rms(o_s, an_ref[...])
    o_full = jnp.concatenate([o_n[h * c:(h + 1) * c] for h in range(nh)], axis=1)
    zz = z_ref[0]
    o_ref[0] = o_full * (zz * jax.nn.sigmoid(zz))

    @pl.when(step == pl.num_programs(1) - 1)
    def _():
        for h in range(nh):
            sout_ref[0, h] = s_new[:, h * A_DK:(h + 1) * A_DK]


def gdn(proj, conv_prev, s0, conv_w, a_log, dt_bias, a_norm, chunk):
    b, t, _ = proj.shape
    nsteps = t // chunk
    prm = jnp.zeros((2, LANES), F32)
    prm = prm.at[0, A_HEADS:2 * A_HEADS].set(a_log).at[1, A_HEADS:2 * A_HEADS].set(dt_bias)
    qkv_w = 3 * A_WIDTH
    return pl.pallas_call(
        functools.partial(_gdn_kernel, chunk=chunk),
        grid=(b, nsteps),
        in_specs=[
            pl.BlockSpec((1, chunk, qkv_w), lambda i, j: (i, j, 0)),
            pl.BlockSpec((1, chunk, A_WIDTH), lambda i, j: (i, j, qkv_w // A_WIDTH)),
            pl.BlockSpec((1, chunk, LANES), lambda i, j: (i, j, (qkv_w + 3 * A_WIDTH) // LANES)),
            pl.BlockSpec((1, CONV_W - 1, qkv_w), lambda i, j: (i, 0, 0)),
            pl.BlockSpec((1, A_HEADS, A_DK, A_DK), lambda i, j: (i, 0, 0, 0)),
            pl.BlockSpec((CONV_W, qkv_w), lambda i, j: (0, 0)),
            pl.BlockSpec((2, LANES), lambda i, j: (0, 0)),
            pl.BlockSpec((1, A_DK), lambda i, j: (0, 0)),
        ],
        out_specs=[
            pl.BlockSpec((1, chunk, A_WIDTH), lambda i, j: (i, j, 0)),
            pl.BlockSpec((1, A_HEADS, A_DK, A_DK), lambda i, j: (i, 0, 0, 0)),
        ],
        out_shape=[
            jax.ShapeDtypeStruct((b, t, A_WIDTH), F32),
            jax.ShapeDtypeStruct((b, A_HEADS, A_DK, A_DK), F32),
        ],
        scratch_shapes=[
            pltpu.VMEM((chunk + 8, qkv_w), F32),
            pltpu.VMEM((A_DK, A_HEADS * A_DK), F32),
        ],
        compiler_params=_cparams(("parallel", "arbitrary")),
        name="gdn",
    )(proj, proj, proj, conv_prev, s0, conv_w, prm, a_norm.reshape(1, A_DK))


def _lru_kernel(xb_ref, gb_ref, cprev_ref, h0_ref, cw_ref, cb_ref, wa_ref, ba_ref, wx_ref, bx_ref, lam_ref,
                ob_ref, hl_ref, xp_ref, a_ref, u_ref, hs_ref, h_ref, *, tb):
    step = pl.program_id(1)

    @pl.when(step == 0)
    def _():
        xp_ref[5:8, :] = cprev_ref[0]
        h_ref[...] = h0_ref[0]

    xp_ref[8:8 + tb, :] = xb_ref[0]
    xc = xp_ref[5:5 + tb, :] * cw_ref[0:1, :]
    for j in range(1, CONV_W):
        xc = xc + xp_ref[5 + j:5 + j + tb, :] * cw_ref[j:j + 1, :]
    xp_ref[5:8, :] = xp_ref[5 + tb:8 + tb, :]
    xc = xc + cb_ref[...]
    xcb = xc.astype(BF16)
    rg = jax.nn.sigmoid(_dot(xcb, wa_ref[...]) + ba_ref[...])
    ig = jax.nn.sigmoid(_dot(xcb, wx_ref[...]) + bx_ref[...])
    log_a = -RG_C * rg * _softplus(-lam_ref[...])
    a_ref[...] = jnp.exp(log_a)
    u_ref[...] = jnp.sqrt(1.0 - jnp.exp(2.0 * log_a)) * (ig * xc)

    def body(i, h):
        h = a_ref[pl.ds(i, 1), :] * h + u_ref[pl.ds(i, 1), :]
        hs_ref[pl.ds(i, 1), :] = h
        return h

    h = lax.fori_loop(0, tb, body, h_ref[...], unroll=8)
    h_ref[...] = h
    ob_ref[0] = hs_ref[...] * jax.nn.gelu(gb_ref[0])

    @pl.when(step == pl.num_programs(1) - 1)
    def _():
        hl_ref[0] = h


def lru(proj, conv_prev, h0, conv_w, conv_b, wa_bd, ba, wx_bd, bx, lam):
    b, t, _ = proj.shape
    tb = _row_tile(t, 256)
    xb_blk = (3 * A_WIDTH + A_WIDTH) // B_WIDTH
    vec = lambda v: v.reshape(1, B_WIDTH)
    const = lambda shape: pl.BlockSpec(shape, lambda i, j: (0,) * len(shape))
    ob, hl = pl.pallas_call(
        functools.partial(_lru_kernel, tb=tb),
        grid=(b, t // tb),
        in_specs=[
            pl.BlockSpec((1, tb, B_WIDTH), lambda i, j: (i, j, xb_blk)),
            pl.BlockSpec((1, tb, B_WIDTH), lambda i, j: (i, j, xb_blk + 1)),
            pl.BlockSpec((1, CONV_W - 1, B_WIDTH), lambda i, j: (i, 0, 0)),
            pl.BlockSpec((1, 1, B_WIDTH), lambda i, j: (i, 0, 0)),
            const((CONV_W, B_WIDTH)), const((1, B_WIDTH)),
            const((B_WIDTH, B_WIDTH)), const((1, B_WIDTH)),
            const((B_WIDTH, B_WIDTH)), const((1, B_WIDTH)),
            const((1, B_WIDTH)),
        ],
        out_specs=[
            pl.BlockSpec((1, tb, B_WIDTH), lambda i, j: (i, j, 0)),
            pl.BlockSpec((1, 1, B_WIDTH), lambda i, j: (i, 0, 0)),
        ],
        out_shape=[
            jax.ShapeDtypeStruct((b, t, B_WIDTH), F32),
            jax.ShapeDtypeStruct((b, 1, B_WIDTH), F32),
        ],
        scratch_shapes=[
            pltpu.VMEM((tb + 8, B_WIDTH), F32),
            pltpu.VMEM((tb, B_WIDTH), F32),
            pltpu.VMEM((tb, B_WIDTH), F32),
            pltpu.VMEM((tb, B_WIDTH), F32),
            pltpu.VMEM((1, B_WIDTH), F32),
        ],
        compiler_params=_cparams(("parallel", "arbitrary")),
        name="lru",
    )(proj, proj, conv_prev, h0.reshape(b, 1, B_WIDTH), conv_w, vec(conv_b), wa_bd, vec(ba), wx_bd, vec(bx), vec(lam))
    return ob, hl.reshape(b, B_WIDTH)


def _attn_kernel(q_ref, kp_ref, vp_ref, kn_ref, vn_ref, lam_ref, ng_ref, o_ref, *, tq, nq, past, lam_init):
    qi = pl.program_id(2)
    q = q_ref[0] * (C_DH ** -0.5)
    lane = lax.broadcasted_iota(jnp.int32, q.shape, 1)
    qc = (jnp.where(lane < C_DH, q, 0.0).astype(BF16), jnp.where(lane >= C_DH, q, 0.0).astype(BF16))
    lv = lam_ref[...]
    lam = (jnp.exp(jnp.sum(lv[0:1] * lv[1:2], axis=-1, keepdims=True))
           - jnp.exp(jnp.sum(lv[2:3] * lv[3:4], axis=-1, keepdims=True)) + lam_init)
    rpos = lax.broadcasted_iota(jnp.int32, (tq, tq), 0)
    cpos = lax.broadcasted_iota(jnp.int32, (tq, tq), 1)
    tail_mask = (cpos // CHUNK) <= (rpos // CHUNK)

    def tile(i):
        p0 = past + i * tq
        kt = kn_ref[0, i * tq:(i + 1) * tq, :].astype(BF16)
        vt = vn_ref[0, i * tq:(i + 1) * tq, :].astype(BF16)
        if p0:
            kf = kp_ref[0, 0:p0, :].astype(BF16)
            vf = vp_ref[0, 0:p0, :].astype(BF16)
        o = None
        for comp in range(2):
            st = jnp.where(tail_mask, _dot_nt(qc[comp], kt), -jnp.inf)
            m = jnp.max(st, axis=-1, keepdims=True)
            if p0:
                sf = _dot_nt(qc[comp], kf)
                m = jnp.maximum(m, jnp.max(sf, axis=-1, keepdims=True))
            et = jnp.exp(st - m)
            l = jnp.sum(et, axis=-1, keepdims=True)
            acc = _dot(et.astype(BF16), vt)
            if p0:
                ef = jnp.exp(sf - m)
                l = l + jnp.sum(ef, axis=-1, keepdims=True)
                acc = acc + _dot(ef.astype(BF16), vf)
            oc = acc / l
            o = oc if comp == 0 else o - lam * oc
        o_ref[0] = _rms(o, ng_ref[...]) * (1.0 - lam_init)

    for i in range(nq):
        pl.when(qi == i)(functools.partial(tile, i))


def diff_attn(qkv, k_past, v_past, lam_vecs, norm_g, lam_init):
    b, t_q, _ = qkv.shape
    past = 0 if k_past is None else k_past.shape[1]
    tq = _row_tile(t_q, 256)
    nq = t_q // tq
    assert past % CHUNK == 0 and (nq == 1 or (tq % CHUNK == 0 and past == 0))
    new_blk = lambda col0: pl.BlockSpec((1, t_q, C_VD), lambda i, h, j: (i, 0, col0 + h))
    if k_past is None:
        pre, pre_specs = (qkv, qkv), [new_blk(C_HEADS), new_blk(2 * C_HEADS)]
    else:
        pre = (k_past, v_past)
        pre_specs = [pl.BlockSpec((1, past, C_VD), lambda i, h, j: (i, 0, h)) for _ in range(2)]
    return pl.pallas_call(
        functools.partial(_attn_kernel, tq=tq, nq=nq, past=past, lam_init=lam_init),
        grid=(b, C_HEADS, nq),
        in_specs=[pl.BlockSpec((1, tq, C_VD), lambda i, h, j: (i, j, h))] + pre_specs + [
            new_blk(C_HEADS), new_blk(2 * C_HEADS),
            pl.BlockSpec((4, C_DH), lambda i, h, j: (0, 0)),
            pl.BlockSpec((1, C_VD), lambda i, h, j: (0, 0)),
        ],
        out_specs=pl.BlockSpec((1, tq, C_VD), lambda i, h, j: (i, j, h)),
        out_shape=jax.ShapeDtypeStruct((b, t_q, C_WIDTH), F32),
        compiler_params=_cparams(("parallel", "parallel", "arbitrary")),
        name="diff_attn",
    )(qkv, *pre, qkv, qkv, lam_vecs, norm_g.reshape(1, C_VD))


def _topk_extract(vals, k, payload=None):
    n_rows = vals.shape[0]
    rio = lax.broadcasted_iota(jnp.int32, vals.shape, 0)
    out_v, out_p = [], []
    for _ in range(k):
        m = jnp.max(vals, axis=0, keepdims=True)
        am = jnp.min(jnp.where(vals == m, rio, n_rows), axis=0, keepdims=True)
        hit = rio == am
        out_v.append(m)
        out_p.append(am if payload is None else jnp.max(jnp.where(hit, payload, -1), axis=0, keepdims=True))
        vals = jnp.where(hit, -jnp.inf, vals)
    return jnp.concatenate(out_v, axis=0), jnp.concatenate(out_p, axis=0), vals


def _topk_rows(vals, k, payload=None):
    return _topk_extract(vals, k, payload)[:2]


_CAND_PER_ROW = [PEER_TOPK // (i + 1) for i in range(PEER_TOPK)]
_N_CAND = sum(_CAND_PER_ROW)
_N_CAND_PAD = -(-_N_CAND // 8) * 8


TOPK_TOKENS = 256


def _subkey_scores(q_ref, keys_ref, h, p):
    half = PEER_DQ // 2
    off = pl.multiple_of(h * PEER_DQ + p * half, half)
    return _dot_nt(keys_ref[p], q_ref[:, pl.ds(off, half)], HI)


def _subkey_top(q_ref, keys_ref, h, p):
    return _topk_rows(_subkey_scores(q_ref, keys_ref, h, p), PEER_TOPK)


def _pair_top(s1, i1, s2, i2, h, cs_ref, ce_ref, idx_t, gate_t):
    tt = s1.shape[1]
    cs_ref[_N_CAND:, :] = jnp.full((_N_CAND_PAD - _N_CAND, tt), -jnp.inf, F32)
    ce_ref[_N_CAND:, :] = jnp.full((_N_CAND_PAD - _N_CAND, tt), -1, jnp.int32)
    row0 = 0
    for i, cnt in enumerate(_CAND_PER_ROW):
        cs_ref[row0:row0 + cnt, :] = s1[i:i + 1] + s2[0:cnt]
        ce_ref[row0:row0 + cnt, :] = i1[i:i + 1] * N_KEYS + i2[0:cnt]
        row0 += cnt
    top_s, top_e = _topk_rows(cs_ref[...], PEER_TOPK, ce_ref[...])
    e = jnp.exp(top_s - top_s[0:1])
    gate = e / jnp.sum(e, axis=0, keepdims=True)
    r0 = pl.multiple_of(h * PEER_TOPK, PEER_TOPK)
    idx_t[pl.ds(r0, PEER_TOPK), :] = top_e
    gate_t[pl.ds(r0, PEER_TOPK), :] = gate


def _topk_scratch(tt):
    return [pltpu.VMEM((NSEL, tt), jnp.int32), pltpu.VMEM((NSEL, tt), F32),
            pltpu.VMEM((_N_CAND_PAD, tt), F32), pltpu.VMEM((_N_CAND_PAD, tt), jnp.int32)]


def _peer_topk_kernel(q_ref, keys_ref, idx_ref, gate_ref, idx_t, gate_t, cs_ref, ce_ref):
    def head(h, carry):
        s1, i1 = _subkey_top(q_ref, keys_ref, h, 0)
        s2, i2 = _subkey_top(q_ref, keys_ref, h, 1)
        _pair_top(s1, i1, s2, i2, h, cs_ref, ce_ref, idx_t, gate_t)
        return carry

    lax.fori_loop(0, PEER_HEADS, head, 0)
    idx_ref[...] = idx_t[...].T
    gate_ref[...] = gate_t[...].T


def peer_topk(qry, sub_keys, n_rows):
    n = n_rows
    tt = _row_tile(n, TOPK_TOKENS)
    nsel = NSEL
    return pl.pallas_call(
        _peer_topk_kernel,
        grid=(n // tt,),
        in_specs=[
            pl.BlockSpec((tt, PEER_HEADS * PEER_DQ), lambda i: (i, 0)),
            pl.BlockSpec((2, N_KEYS, PEER_DQ // 2), lambda i: (0, 0, 0)),
        ],
        out_specs=[
            pl.BlockSpec((tt, nsel), lambda i: (i, 0)),
            pl.BlockSpec((tt, nsel), lambda i: (i, 0)),
        ],
        out_shape=[
            jax.ShapeDtypeStruct((n, nsel), jnp.int32),
            jax.ShapeDtypeStruct((n, nsel), F32),
        ],
        scratch_shapes=_topk_scratch(tt),
        compiler_params=_cparams(("parallel",)),
        name="peer_topk",
    )(qry, sub_keys)


PEER_GROUP = 8
NSEL = PEER_HEADS * PEER_TOPK
SLAB_ROWS = 8
SLAB_PITCH = 12


def peer_table(u, v):
    ne, d = u.shape
    assert d == SLAB_ROWS * LANES
    hi = lax.bitcast_convert_type(u.astype(BF16), jnp.uint16).astype(jnp.uint32)
    lo = lax.bitcast_convert_type(v.astype(BF16), jnp.uint16).astype(jnp.uint32)
    return ((hi << 16) | lo).reshape(ne, SLAB_ROWS, LANES)


SIDE_SLICES = 2 * PEER_HEADS


def _peer_expert_kernel(*refs, final_norm, side):
    if side:
        (idx_ref, idxn_ref, x_ref, gate_ref, g_ref, gf_ref, uv_hbm, qn_ref, keys_ref,
         o_ref, idxo_ref, gateo_ref, gbuf0, gbuf1, sem, idx_t, gate_t, cs_ref, ce_ref, s_keep, i_keep,
         s_half, i_half, rest_ref) = refs
    else:
        idx_ref, idxn_ref, x_ref, gate_ref, g_ref, gf_ref, uv_hbm, o_ref, gbuf0, gbuf1, sem = refs
    g = PEER_GROUP
    d = x_ref.shape[1]
    nsub = d // LANES
    step = pl.program_id(0)
    gbufs = (gbuf0, gbuf1)

    half_k = PEER_TOPK // 2

    def side_first_half():
        k = step % SIDE_SLICES
        sc = _subkey_scores(qn_ref, keys_ref, k // 2, k % 2)
        s, i, rest = _topk_extract(sc, half_k)
        s_half[...] = s
        i_half[...] = i
        rest_ref[...] = rest

    def side_second_half():
        k = step % SIDE_SLICES
        h = k // 2
        p = k % 2
        s, i, _ = _topk_extract(rest_ref[...], half_k)
        s = jnp.concatenate([s_half[...], s], axis=0)
        i = jnp.concatenate([i_half[...], i], axis=0)

        @pl.when(p == 0)
        def _():
            s_keep[...] = s
            i_keep[...] = i

        @pl.when(p == 1)
        def _():
            _pair_top(s_keep[...], i_keep[...], s, i, h, cs_ref, ce_ref, idx_t, gate_t)

        @pl.when(k == SIDE_SLICES - 1)
        def _():
            idxo_ref[...] = idx_t[...].T
            gateo_ref[...] = gate_t[...].T

    def slab_copy(e, slot, t, r):
        row0 = (t * NSEL + r) * SLAB_PITCH
        return pltpu.make_async_copy(uv_hbm.at[e], gbufs[slot].at[pl.ds(row0, SLAB_ROWS), :], sem.at[slot])

    def issue_token(iref, t_src, slot, t):
        for r in range(NSEL):
            slab_copy(iref[t_src, r], slot, t, r).start(priority=r % 2)

    def wait_group(slot):
        def body(t, c):
            for r in range(NSEL):
                slab_copy(0, slot, 0, 0).wait()
            return c
        lax.fori_loop(0, g, body, 0)

    def expert_rows(slot, t):
        base = t * NSEL * SLAB_PITCH
        words = jnp.concatenate(
            [gbufs[slot][pl.ds(base + s, NSEL, stride=SLAB_PITCH), :] for s in range(nsub)], axis=1)
        u = lax.bitcast_convert_type(words & jnp.uint32(0xFFFF0000), F32).astype(BF16)
        v = lax.bitcast_convert_type(words << 16, F32).astype(BF16)
        return u, v

    def combine_group(row0, slot, iref, t_src0, slot_next):
        x = x_ref[row0:row0 + g, :]
        xnb = _rms(x, g_ref[...]).astype(BF16)
        gate = gate_ref[row0:row0 + g, :]
        rio = lax.broadcasted_iota(jnp.int32, (g, NSEL), 0)
        out = x
        for t in range(g):
            issue_token(iref, t_src0 + t, slot_next, t)
            u, v = expert_rows(slot, t)
            act = _dot_nt(xnb, u)
            w = jnp.where(rio == t, gate * jax.nn.gelu(act), 0.0).astype(BF16)
            out = out + _dot(w, v)
        if final_norm:
            out = _rms(out, gf_ref[...])
        o_ref[row0:row0 + g, :] = out

    @pl.when(step == 0)
    def _():
        for t in range(g):
            issue_token(idx_ref, t, 0, t)

    if side:
        side_first_half()
    wait_group(0)
    combine_group(0, 0, idx_ref, g, 1)
    if side:
        side_second_half()
    wait_group(1)
    combine_group(g, 1, idxn_ref, 0, 0)

    @pl.when(step == pl.num_programs(0) - 1)
    def _():
        wait_group(0)


def peer_experts(x, idx, gate, norm_g, final_g, uv, final_norm, row0=0, side=None):
    n = idx.shape[0]
    d = x.shape[1]
    g = PEER_GROUP
    step_rows = 2 * g
    nsteps = n // step_rows
    ngroups = 2 * nsteps
    assert n % step_rows == 0 and row0 % step_rows == 0
    blk0 = row0 // step_rows
    in_specs = [
        pl.BlockSpec((step_rows, NSEL), lambda i: (i, 0), memory_space=pltpu.SMEM),
        pl.BlockSpec((g, NSEL), lambda i: (jnp.minimum(2 * i + 2, ngroups - 1), 0), memory_space=pltpu.SMEM),
        pl.BlockSpec((step_rows, d), lambda i: (i + blk0, 0)),
        pl.BlockSpec((step_rows, NSEL), lambda i: (i, 0)),
        pl.BlockSpec((1, d), lambda i: (0, 0)),
        pl.BlockSpec((1, d), lambda i: (0, 0)),
        pl.BlockSpec(memory_space=pl.ANY),
    ]
    args = [idx, idx, x, gate, norm_g.reshape(1, d), final_g.reshape(1, d), uv]
    out_specs = [pl.BlockSpec((step_rows, d), lambda i: (i + blk0, 0))]
    out_shape = [jax.ShapeDtypeStruct(x.shape, F32)]
    scratch = [
        pltpu.VMEM((g * NSEL * SLAB_PITCH, LANES), jnp.uint32),
        pltpu.VMEM((g * NSEL * SLAB_PITCH, LANES), jnp.uint32),
        pltpu.SemaphoreType.DMA((2,)),
    ]
    if side is not None:
        qry, sub_keys, side_row0 = side
        tt = TOPK_TOKENS
        assert tt == SIDE_SLICES * step_rows and n % tt == 0 and side_row0 % tt == 0
        sblk0 = side_row0 // tt
        in_specs += [
            pl.BlockSpec((tt, qry.shape[1]), lambda i: (i // SIDE_SLICES + sblk0, 0)),
            pl.BlockSpec(sub_keys.shape, lambda i: (0, 0, 0)),
        ]
        args += [qry, sub_keys]
        out_specs += [pl.BlockSpec((tt, NSEL), lambda i: (i // SIDE_SLICES, 0)),
                      pl.BlockSpec((tt, NSEL), lambda i: (i // SIDE_SLICES, 0))]
        out_shape += [jax.ShapeDtypeStruct((n, NSEL), jnp.int32), jax.ShapeDtypeStruct((n, NSEL), F32)]
        scratch += _topk_scratch(tt) + [
            pltpu.VMEM((PEER_TOPK, tt), F32), pltpu.VMEM((PEER_TOPK, tt), jnp.int32),
            pltpu.VMEM((PEER_TOPK // 2, tt), F32), pltpu.VMEM((PEER_TOPK // 2, tt), jnp.int32),
            pltpu.VMEM((N_KEYS, tt), F32)]
    res = pl.pallas_call(
        functools.partial(_peer_expert_kernel, final_norm=final_norm, side=side is not None),
        grid=(nsteps,),
        in_specs=in_specs,
        out_specs=out_specs,
        out_shape=out_shape,
        scratch_shapes=scratch,
        input_output_aliases={2: 0},
        compiler_params=_cparams(("arbitrary",)),
        name="peer_experts",
    )(*args)
    return res if side is not None else res[0]


def _block_diag(w):
    nb, bi, bj = w.shape
    out = jnp.zeros((nb * bi, nb * bj), w.dtype)
    for i in range(nb):
        out = out.at[i * bi:(i + 1) * bi, i * bj:(i + 1) * bj].set(w[i])
    return out


def _prep_weights(p):
    depth = p['norm_mix'].shape[0]
    d = p['norm_final'].shape[0]
    w = {}
    a3, a4 = 3 * A_WIDTH, 4 * A_WIDTH
    w['ab_in'], w['ab_out_a'], w['ab_out_b'], w['ga'], w['gx'] = [], [], [], [], []
    for j in range(p['ab_w_in'].shape[0]):
        wi = p['ab_w_in'][j]
        pad = jnp.zeros((d, LANES - 2 * A_HEADS), wi.dtype)
        cols = [wi[:, :a4], wi[:, a4 + 2 * A_HEADS:], wi[:, a4:a4 + 2 * A_HEADS], pad]
        w['ab_in'].append(jnp.concatenate(cols, axis=1).astype(BF16))
        w['ab_out_a'].append(p['ab_w_out'][j][:A_WIDTH].astype(BF16))
        w['ab_out_b'].append(p['ab_w_out'][j][A_WIDTH:].astype(BF16))
        w['ga'].append(_block_diag(p['b_gate_a_w'][j]).astype(BF16))
        w['gx'].append(_block_diag(p['b_gate_x_w'][j]).astype(BF16))
    w['c_qkv'] = [m.astype(BF16) for m in p['c_w_qkv']]
    w['c_out'] = [m.astype(BF16) for m in p['c_w_out']]
    w['c_lam'] = [jnp.stack([p['c_lambda_q1'][j], p['c_lambda_k1'][j], p['c_lambda_q2'][j], p['c_lambda_k2'][j]])
                  for j in range(p['c_w_qkv'].shape[0])]
    w['peer_q'] = [p['peer_w_query'][l].astype(BF16) for l in range(depth)]
    w['peer_uv'] = [peer_table(p['peer_u'][l], p['peer_v'][l]) for l in range(depth)]
    return w


PEER_PARTS = 4


def _peer(x2, l, p, w, final_norm):
    n = x2.shape[0]
    keys = p['peer_sub_keys'][l]
    qry = norm_matmul(x2, p['norm_ffn'][l], w['peer_q'][l])
    parts = PEER_PARTS if n % (PEER_PARTS * TOPK_TOKENS) == 0 else 1
    n_part = n // parts
    idx, gate = peer_topk(qry, keys, n_part)
    for i in range(parts):
        side = (qry, keys, (i + 1) * n_part) if i + 1 < parts else None
        res = peer_experts(x2, idx, gate, p['norm_ffn'][l], p['norm_final'], w['peer_uv'][l], final_norm,
                           row0=i * n_part, side=side)
        x2, idx, gate = res if side is not None else (res, None, None)
    return x2


def _trunk(x, st_a_conv, st_a_delta, st_b_conv, st_b_h, c_k, c_v, p, w, past):
    b, t, d = x.shape
    n = b * t
    chunk = min(CHUNK, t)
    depth = p['norm_mix'].shape[0]
    x2 = x.reshape(n, d)
    a_conv, a_delta, b_conv, b_h, new_k, new_v = [], [], [], [], [], []
    for l in range(depth):
        j = l // 2
        if l % 2 == 0:
            proj = norm_matmul(x2, p['norm_mix'][l], w['ab_in'][j]).reshape(b, t, -1)
            o_a, s_new = gdn(proj, st_a_conv[j], st_a_delta[j], p['a_conv_w'][j], p['a_log'][j],
                             p['a_dt_bias'][j], p['a_norm'][j], chunk)
            o_b, h_last = lru(proj, st_b_conv[j], st_b_h[j], p['b_conv_w'][j], p['b_conv_b'][j], w['ga'][j],
                              p['b_gate_a_b'][j], w['gx'][j], p['b_gate_x_b'][j], p['b_lambda'][j])
            a3 = 3 * A_WIDTH
            xb0 = a3 + A_WIDTH
            a_conv.append(jnp.concatenate([st_a_conv[j], proj[:, :, :a3]], axis=1)[:, -(CONV_W - 1):])
            b_conv.append(jnp.concatenate([st_b_conv[j], proj[:, :, xb0:xb0 + B_WIDTH]], axis=1)[:, -(CONV_W - 1):])
            a_delta.append(s_new)
            b_h.append(h_last)
            x2 = matmul_res([o_a.reshape(n, A_WIDTH), o_b.reshape(n, B_WIDTH)],
                            [w['ab_out_a'][j], w['ab_out_b'][j]], x2)
        else:
            lam_init = 0.8 - 0.6 * math.exp(-0.3 * l)
            qkv = norm_matmul(x2, p['norm_mix'][l], w['c_qkv'][j]).reshape(b, t, 3 * C_WIDTH)
            kn = qkv[:, :, C_WIDTH:2 * C_WIDTH]
            vn = qkv[:, :, 2 * C_WIDTH:]
            kp = None if c_k is None else c_k[j].reshape(b, -1, C_WIDTH)
            vp = None if c_v is None else c_v[j].reshape(b, -1, C_WIDTH)
            assert past == (0 if kp is None else kp.shape[1])
            o_c = diff_attn(qkv, kp, vp, w['c_lam'][j], p['c_norm'][j], lam_init)
            new_k.append(kn.reshape(b, t, C_HEADS, 2, C_DH))
            new_v.append(vn.reshape(b, t, C_HEADS, C_VD))
            x2 = matmul_res([o_c.reshape(n, C_WIDTH)], [w['c_out'][j]], x2)
        x2 = _peer(x2, l, p, w, final_norm=(l == depth - 1))
    return (x2.reshape(b, t, d), jnp.stack(a_conv), jnp.stack(a_delta), jnp.stack(b_conv), jnp.stack(b_h),
            jnp.stack(new_k), jnp.stack(new_v))


def kernel(x_prompt, x_sample, state_a_conv, state_a_delta, state_b_conv, state_b_rglru, cache_c_k, cache_c_v, norm_mix, norm_ffn, norm_final, ab_w_in, ab_w_out, a_conv_w, a_log, a_dt_bias, a_norm, b_conv_w, b_conv_b, b_gate_a_w, b_gate_a_b, b_gate_x_w, b_gate_x_b, b_lambda, c_w_qkv, c_w_out, c_lambda_q1, c_lambda_k1, c_lambda_q2, c_lambda_k2, c_norm, peer_w_query, peer_sub_keys, peer_u, peer_v):
    p = dict(norm_mix=norm_mix, norm_ffn=norm_ffn, norm_final=norm_final, ab_w_in=ab_w_in,
             ab_w_out=ab_w_out, a_conv_w=a_conv_w, a_log=a_log, a_dt_bias=a_dt_bias, a_norm=a_norm,
             b_conv_w=b_conv_w, b_conv_b=b_conv_b, b_gate_a_w=b_gate_a_w, b_gate_a_b=b_gate_a_b,
             b_gate_x_w=b_gate_x_w, b_gate_x_b=b_gate_x_b, b_lambda=b_lambda, c_w_qkv=c_w_qkv,
             c_w_out=c_w_out, c_lambda_q1=c_lambda_q1, c_lambda_k1=c_lambda_k1,
             c_lambda_q2=c_lambda_q2, c_lambda_k2=c_lambda_k2, c_norm=c_norm,
             peer_w_query=peer_w_query, peer_sub_keys=peer_sub_keys, peer_u=peer_u, peer_v=peer_v)
    w = _prep_weights(p)
    bp = x_prompt.shape[0]
    dt = x_prompt.dtype
    n_even = state_a_conv.shape[0]
    z_a_conv = jnp.zeros((n_even, bp) + state_a_conv.shape[2:], dt)
    z_a_delta = jnp.zeros((n_even, bp) + state_a_delta.shape[2:], dt)
    z_b_conv = jnp.zeros((n_even, bp) + state_b_conv.shape[2:], dt)
    z_b_h = jnp.zeros((n_even, bp) + state_b_rglru.shape[2:], dt)
    (y_prompt, pa_conv, pa_delta, pb_conv, pb_h, pc_k, pc_v) = _trunk(
        x_prompt, z_a_conv, z_a_delta, z_b_conv, z_b_h, None, None, p, w, 0)
    (y_sample, sa_conv, sa_delta, sb_conv, sb_h, sc_k, sc_v) = _trunk(
        x_sample, state_a_conv, state_a_delta, state_b_conv, state_b_rglru, cache_c_k, cache_c_v,
        p, w, cache_c_k.shape[2])
    return (y_prompt, y_sample, pa_conv, pa_delta, pb_conv, pb_h, pc_k, pc_v,
            sa_conv, sa_delta, sb_conv, sb_h, sc_k, sc_v)
```

```python
import functools
import math

import jax
import jax.numpy as jnp
from jax import lax
from jax.experimental import pallas as pl
from jax.experimental.pallas import tpu as pltpu

F32 = jnp.float32
BF16 = jnp.bfloat16
HI = lax.Precision.HIGHEST

EPS = 1e-6
CHUNK = 64
CONV_W = 4
A_HEADS = 4
A_DK = 128
A_WIDTH = A_HEADS * A_DK
B_WIDTH = 512
B_BLOCKS = 8
RG_C = 8.0
C_HEADS = 8
C_DH = 64
C_VD = 2 * C_DH
C_WIDTH = C_HEADS * C_VD
PEER_HEADS = 8
N_KEYS = 128
PEER_TOPK = 16
PEER_DQ = 256
LANES = 128
VMEM_LIMIT = 48 * 1024 * 1024


def _cparams(sem):
    return pltpu.CompilerParams(dimension_semantics=sem, vmem_limit_bytes=VMEM_LIMIT)


def _dot(a, b, prec=None):
    return jnp.dot(a, b, precision=prec, preferred_element_type=F32)


def _dot_nt(a, b, prec=None):
    return lax.dot_general(a, b, (((1,), (1,)), ((), ())), precision=prec, preferred_element_type=F32)


def _dot_bf16(a, b):
    return _dot(a.astype(BF16), b.astype(BF16))


def _dot_sel(sel, b, nt=False):
    f = _dot_nt if nt else _dot
    s = sel.astype(F32).astype(BF16)
    b1 = b.astype(BF16)
    r1 = b - b1.astype(F32)
    b2 = r1.astype(BF16)
    b3 = (r1 - b2.astype(F32)).astype(BF16)
    return f(s, b1) + f(s, b2) + f(s, b3)


def _rms(x, g):
    return x * lax.rsqrt(jnp.mean(x * x, axis=-1, keepdims=True) + EPS) * g


def _softplus(x):
    return jnp.maximum(x, 0.0) + jnp.log1p(jnp.exp(-jnp.abs(x)))


def _row_tile(n, pref):
    t = min(n, pref)
    while n % t:
        t //= 2
    return t


def _norm_matmul_kernel(x_ref, g_ref, w_ref, o_ref):
    y = _rms(x_ref[...], g_ref[...])
    o_ref[...] = _dot(y.astype(BF16), w_ref[...])


def norm_matmul(x, g, w):
    n, d = x.shape
    m = w.shape[1]
    tm = _row_tile(n, 256)
    return pl.pallas_call(
        _norm_matmul_kernel,
        grid=(n // tm,),
        in_specs=[
            pl.BlockSpec((tm, d), lambda i: (i, 0)),
            pl.BlockSpec((1, d), lambda i: (0, 0)),
            pl.BlockSpec((d, m), lambda i: (0, 0)),
        ],
        out_specs=pl.BlockSpec((tm, m), lambda i: (i, 0)),
        out_shape=jax.ShapeDtypeStruct((n, m), F32),
        compiler_params=_cparams(("parallel",)),
        name="norm_matmul",
    )(x, g.reshape(1, d), w)


def _matmul_res_kernel(*refs, n_in):
    ys = refs[:n_in]
    ws = refs[n_in:2 * n_in]
    res_ref = refs[2 * n_in]
    o_ref = refs[2 * n_in + 1]
    acc = res_ref[...]
    for y_ref, w_ref in zip(ys, ws):
        acc = acc + _dot(y_ref[...].astype(BF16), w_ref[...])
    o_ref[...] = acc


def matmul_res(ys, ws, res):
    n, d = res.shape
    tm = _row_tile(n, 512)
    n_in = len(ys)
    in_specs = [pl.BlockSpec((tm, y.shape[1]), lambda i: (i, 0)) for y in ys]
    in_specs += [pl.BlockSpec(w.shape, lambda i: (0, 0)) for w in ws]
    in_specs += [pl.BlockSpec((tm, d), lambda i: (i, 0))]
    return pl.pallas_call(
        functools.partial(_matmul_res_kernel, n_in=n_in),
        grid=(n // tm,),
        in_specs=in_specs,
        out_specs=pl.BlockSpec((tm, d), lambda i: (i, 0)),
        out_shape=jax.ShapeDtypeStruct((n, d), F32),
        compiler_params=_cparams(("parallel",)),
        name="matmul_res",
    )(*ys, *ws, res)


def _gdn_kernel(qkv_ref, z_ref, ba_ref, cprev_ref, s0_ref, cw_ref, prm_ref, an_ref,
                o_ref, sout_ref, xp_ref, s_ref, *, chunk):
    c = chunk
    nh = A_HEADS
    r = nh * c
    step = pl.program_id(1)

    @pl.when(step == 0)
    def _():
        xp_ref[5:8, :] = cprev_ref[0]
        for h in range(nh):
            s_ref[:, h * A_DK:(h + 1) * A_DK] = s0_ref[0, h]

    xp_ref[8:8 + c, :] = qkv_ref[0]
    conv = xp_ref[5:5 + c, :] * cw_ref[0:1, :]
    for j in range(1, CONV_W):
        conv = conv + xp_ref[5 + j:5 + j + c, :] * cw_ref[j:j + 1, :]
    xp_ref[5:8, :] = xp_ref[5 + c:8 + c, :]
    act = conv * jax.nn.sigmoid(conv)

    def stack(off):
        return jnp.concatenate([act[:, off + h * A_DK: off + (h + 1) * A_DK] for h in range(nh)], axis=0)

    qs, ks, vs = stack(0), stack(A_WIDTH), stack(2 * A_WIDTH)
    qs = qs * lax.rsqrt(jnp.sum(qs * qs, axis=-1, keepdims=True) + EPS) * (A_DK ** -0.5)
    ks = ks * lax.rsqrt(jnp.sum(ks * ks, axis=-1, keepdims=True) + EPS)

    ba = ba_ref[0]
    beta_full = jax.nn.sigmoid(ba)
    g_full = -jnp.exp(prm_ref[0:1, :]) * _softplus(ba + prm_ref[1:2, :])
    beta_s = jnp.concatenate([beta_full[:, h:h + 1] for h in range(nh)], axis=0)
    g_s = jnp.concatenate([g_full[:, nh + h:nh + h + 1] for h in range(nh)], axis=0)

    row = lax.broadcasted_iota(jnp.int32, (r, r), 0)
    col = lax.broadcasted_iota(jnp.int32, (r, r), 1)
    same = (row // c) == (col // c)
    m_incl = same & (col <= row)
    m_strict = same & (col < row)

    gcs = _dot_sel(m_incl, jnp.broadcast_to(g_s, (r, LANES)))
    lane0 = lax.broadcasted_iota(jnp.int32, (r, LANES), 1) == 0
    gcs_row = _dot_sel(lane0, gcs, nt=True)
    gc = gcs[:, 0:1]
    decay = jnp.exp(jnp.where(m_incl, gc - gcs_row, -jnp.inf))

    kb = ks * beta_s
    ks_b = ks.astype(BF16)
    a_mat = jnp.where(m_strict, _dot_nt(kb.astype(BF16), ks_b), 0.0) * decay
    eye = (row == col).astype(F32)
    p = -a_mat
    t_inv = eye + p
    for _ in range(int(math.log2(c)) - 1):
        p = _dot_bf16(p, p)
        t_inv = t_inv + _dot_bf16(t_inv, p)
    uw = _dot_bf16(t_inv, jnp.concatenate([vs * beta_s, kb * jnp.exp(gc)], axis=1))
    u, w = uw[:, :A_DK], uw[:, A_DK:]
    qk = jnp.where(m_incl, _dot_nt(qs.astype(BF16), ks_b), 0.0) * decay

    def diag_blocks(x):
        return jnp.concatenate([x[h * c:(h + 1) * c, h * A_DK:(h + 1) * A_DK] for h in range(nh)], axis=0)

    s_cat = s_ref[...]
    wq_s = _dot_bf16(jnp.concatenate([w, qs * jnp.exp(gc)], axis=0), s_cat)
    v_new = u - diag_blocks(wq_s[:r])
    o_s = diag_blocks(wq_s[r:]) + _dot_bf16(qk, v_new)

    g_last = jnp.concatenate(
        [jnp.broadcast_to(gcs[(h + 1) * c - 1:(h + 1) * c, :], (c, LANES)) for h in range(nh)], axis=0)
    kdec = ks * jnp.exp(g_last - gcs)
    zero = jnp.zeros((c, A_DK), F32)
    v_exp = jnp.concatenate(
        [jnp.concatenate([v_new[h * c:(h + 1) * c] if hh == h else zero for hh in range(nh)], axis=1)
         for h in range(nh)], axis=0)
    s_scale = jnp.concatenate(
        [jnp.exp(jnp.broadcast_to(gcs[(h + 1) * c - 1:(h + 1) * c, :], (A_DK, LANES))) for h in range(nh)], axis=1)
    s_new = s_cat * s_scale + _dot_bf16(kdec.T, v_exp)
    s_ref[...] = s_new

    o_n = _rms(o_s, an_ref[...])
    o_full = jnp.concatenate([o_n[h * c:(h + 1) * c] for h in range(nh)], axis=1)
    zz = z_ref[0]
    o_ref[0] = o_full * (zz * jax.nn.sigmoid(zz))

    @pl.when(step == pl.num_programs(1) - 1)
    def _():
        for h in range(nh):
            sout_ref[0, h] = s_new[:, h * A_DK:(h + 1) * A_DK]


def gdn(proj, conv_prev, s0, conv_w, a_log, dt_bias, a_norm, chunk):
    b, t, _ = proj.shape
    nsteps = t // chunk
    prm = jnp.zeros((2, LANES), F32)
    prm = prm.at[0, A_HEADS:2 * A_HEADS].set(a_log).at[1, A_HEADS:2 * A_HEADS].set(dt_bias)
    qkv_w = 3 * A_WIDTH
    return pl.pallas_call(
        functools.partial(_gdn_kernel, chunk=chunk),
        grid=(b, nsteps),
        in_specs=[
            pl.BlockSpec((1, chunk, qkv_w), lambda i, j: (i, j, 0)),
            pl.BlockSpec((1, chunk, A_WIDTH), lambda i, j: (i, j, qkv_w // A_WIDTH)),
            pl.BlockSpec((1, chunk, LANES), lambda i, j: (i, j, (qkv_w + 3 * A_WIDTH) // LANES)),
            pl.BlockSpec((1, CONV_W - 1, qkv_w), lambda i, j: (i, 0, 0)),
            pl.BlockSpec((1, A_HEADS, A_DK, A_DK), lambda i, j: (i, 0, 0, 0)),
            pl.BlockSpec((CONV_W, qkv_w), lambda i, j: (0, 0)),
            pl.BlockSpec((2, LANES), lambda i, j: (0, 0)),
            pl.BlockSpec((1, A_DK), lambda i, j: (0, 0)),
        ],
        out_specs=[
            pl.BlockSpec((1, chunk, A_WIDTH), lambda i, j: (i, j, 0)),
            pl.BlockSpec((1, A_HEADS, A_DK, A_DK), lambda i, j: (i, 0, 0, 0)),
        ],
        out_shape=[
            jax.ShapeDtypeStruct((b, t, A_WIDTH), F32),
            jax.ShapeDtypeStruct((b, A_HEADS, A_DK, A_DK), F32),
        ],
        scratch_shapes=[
            pltpu.VMEM((chunk + 8, qkv_w), F32),
            pltpu.VMEM((A_DK, A_HEADS * A_DK), F32),
        ],
        compiler_params=_cparams(("parallel", "arbitrary")),
        name="gdn",
    )(proj, proj, proj, conv_prev, s0, conv_w, prm, a_norm.reshape(1, A_DK))


def _lru_kernel(xb_ref, gb_ref, cprev_ref, h0_ref, cw_ref, cb_ref, wa_ref, ba_ref, wx_ref, bx_ref, lam_ref,
                ob_ref, hl_ref, xp_ref, a_ref, u_ref, hs_ref, h_ref, *, tb):
    step = pl.program_id(1)

    @pl.when(step == 0)
    def _():
        xp_ref[5:8, :] = cprev_ref[0]
        h_ref[...] = h0_ref[0]

    xp_ref[8:8 + tb, :] = xb_ref[0]
    xc = xp_ref[5:5 + tb, :] * cw_ref[0:1, :]
    for j in range(1, CONV_W):
        xc = xc + xp_ref[5 + j:5 + j + tb, :] * cw_ref[j:j + 1, :]
    xp_ref[5:8, :] = xp_ref[5 + tb:8 + tb, :]
    xc = xc + cb_ref[...]
    xcb = xc.astype(BF16)
    rg = jax.nn.sigmoid(_dot(xcb, wa_ref[...]) + ba_ref[...])
    ig = jax.nn.sigmoid(_dot(xcb, wx_ref[...]) + bx_ref[...])
    log_a = -RG_C * rg * _softplus(-lam_ref[...])
    a_ref[...] = jnp.exp(log_a)
    u_ref[...] = jnp.sqrt(1.0 - jnp.exp(2.0 * log_a)) * (ig * xc)

    def body(i, h):
        h = a_ref[pl.ds(i, 1), :] * h + u_ref[pl.ds(i, 1), :]
        hs_ref[pl.ds(i, 1), :] = h
        return h

    h = lax.fori_loop(0, tb, body, h_ref[...], unroll=8)
    h_ref[...] = h
    ob_ref[0] = hs_ref[...] * jax.nn.gelu(gb_ref[0])

    @pl.when(step == pl.num_programs(1) - 1)
    def _():
        hl_ref[0] = h


def lru(proj, conv_prev, h0, conv_w, conv_b, wa_bd, ba, wx_bd, bx, lam):
    b, t, _ = proj.shape
    tb = _row_tile(t, 256)
    xb_blk = (3 * A_WIDTH + A_WIDTH) // B_WIDTH
    vec = lambda v: v.reshape(1, B_WIDTH)
    const = lambda shape: pl.BlockSpec(shape, lambda i, j: (0,) * len(shape))
    ob, hl = pl.pallas_call(
        functools.partial(_lru_kernel, tb=tb),
        grid=(b, t // tb),
        in_specs=[
            pl.BlockSpec((1, tb, B_WIDTH), lambda i, j: (i, j, xb_blk)),
            pl.BlockSpec((1, tb, B_WIDTH), lambda i, j: (i, j, xb_blk + 1)),
            pl.BlockSpec((1, CONV_W - 1, B_WIDTH), lambda i, j: (i, 0, 0)),
            pl.BlockSpec((1, 1, B_WIDTH), lambda i, j: (i, 0, 0)),
            const((CONV_W, B_WIDTH)), const((1, B_WIDTH)),
            const((B_WIDTH, B_WIDTH)), const((1, B_WIDTH)),
            const((B_WIDTH, B_WIDTH)), const((1, B_WIDTH)),
            const((1, B_WIDTH)),
        ],
        out_specs=[
            pl.BlockSpec((1, tb, B_WIDTH), lambda i, j: (i, j, 0)),
            pl.BlockSpec((1, 1, B_WIDTH), lambda i, j: (i, 0, 0)),
        ],
        out_shape=[
            jax.ShapeDtypeStruct((b, t, B_WIDTH), F32),
            jax.ShapeDtypeStruct((b, 1, B_WIDTH), F32),
        ],
        scratch_shapes=[
            pltpu.VMEM((tb + 8, B_WIDTH), F32),
            pltpu.VMEM((tb, B_WIDTH), F32),
            pltpu.VMEM((tb, B_WIDTH), F32),
            pltpu.VMEM((tb, B_WIDTH), F32),
            pltpu.VMEM((1, B_WIDTH), F32),
        ],
        compiler_params=_cparams(("parallel", "arbitrary")),
        name="lru",
    )(proj, proj, conv_prev, h0.reshape(b, 1, B_WIDTH), conv_w, vec(conv_b), wa_bd, vec(ba), wx_bd, vec(bx), vec(lam))
    return ob, hl.reshape(b, B_WIDTH)


def _attn_kernel(q_ref, kp_ref, vp_ref, kn_ref, vn_ref, lam_ref, ng_ref, o_ref, *, tq, nq, past, lam_init):
    qi = pl.program_id(2)
    q = q_ref[0] * (C_DH ** -0.5)
    lane = lax.broadcasted_iota(jnp.int32, q.shape, 1)
    qc = (jnp.where(lane < C_DH, q, 0.0).astype(BF16), jnp.where(lane >= C_DH, q, 0.0).astype(BF16))
    lv = lam_ref[...]
    lam = (jnp.exp(jnp.sum(lv[0:1] * lv[1:2], axis=-1, keepdims=True))
           - jnp.exp(jnp.sum(lv[2:3] * lv[3:4], axis=-1, keepdims=True)) + lam_init)
    rpos = lax.broadcasted_iota(jnp.int32, (tq, tq), 0)
    cpos = lax.broadcasted_iota(jnp.int32, (tq, tq), 1)
    tail_mask = (cpos // CHUNK) <= (rpos // CHUNK)

    def tile(i):
        p0 = past + i * tq
        kt = kn_ref[0, i * tq:(i + 1) * tq, :].astype(BF16)
        vt = vn_ref[0, i * tq:(i + 1) * tq, :].astype(BF16)
        if p0:
            kf = kp_ref[0, 0:p0, :].astype(BF16)
            vf = vp_ref[0, 0:p0, :].astype(BF16)
        o = None
        for comp in range(2):
            st = jnp.where(tail_mask, _dot_nt(qc[comp], kt), -jnp.inf)
            m = jnp.max(st, axis=-1, keepdims=True)
            if p0:
                sf = _dot_nt(qc[comp], kf)
                m = jnp.maximum(m, jnp.max(sf, axis=-1, keepdims=True))
            et = jnp.exp(st - m)
            l = jnp.sum(et, axis=-1, keepdims=True)
            acc = _dot(et.astype(BF16), vt)
            if p0:
                ef = jnp.exp(sf - m)
                l = l + jnp.sum(ef, axis=-1, keepdims=True)
                acc = acc + _dot(ef.astype(BF16), vf)
            oc = acc / l
            o = oc if comp == 0 else o - lam * oc
        o_ref[0] = _rms(o, ng_ref[...]) * (1.0 - lam_init)

    for i in range(nq):
        pl.when(qi == i)(functools.partial(tile, i))


def diff_attn(qkv, k_past, v_past, lam_vecs, norm_g, lam_init):
    b, t_q, _ = qkv.shape
    past = 0 if k_past is None else k_past.shape[1]
    tq = _row_tile(t_q, 256)
    nq = t_q // tq
    assert past % CHUNK == 0 and (nq == 1 or (tq % CHUNK == 0 and past == 0))
    new_blk = lambda col0: pl.BlockSpec((1, t_q, C_VD), lambda i, h, j: (i, 0, col0 + h))
    if k_past is None:
        pre, pre_specs = (qkv, qkv), [new_blk(C_HEADS), new_blk(2 * C_HEADS)]
    else:
        pre = (k_past, v_past)
        pre_specs = [pl.BlockSpec((1, past, C_VD), lambda i, h, j: (i, 0, h)) for _ in range(2)]
    return pl.pallas_call(
        functools.partial(_attn_kernel, tq=tq, nq=nq, past=past, lam_init=lam_init),
        grid=(b, C_HEADS, nq),
        in_specs=[pl.BlockSpec((1, tq, C_VD), lambda i, h, j: (i, j, h))] + pre_specs + [
            new_blk(C_HEADS), new_blk(2 * C_HEADS),
            pl.BlockSpec((4, C_DH), lambda i, h, j: (0, 0)),
            pl.BlockSpec((1, C_VD), lambda i, h, j: (0, 0)),
        ],
        out_specs=pl.BlockSpec((1, tq, C_VD), lambda i, h, j: (i, j, h)),
        out_shape=jax.ShapeDtypeStruct((b, t_q, C_WIDTH), F32),
        compiler_params=_cparams(("parallel", "parallel", "arbitrary")),
        name="diff_attn",
    )(qkv, *pre, qkv, qkv, lam_vecs, norm_g.reshape(1, C_VD))


def _topk_extract(vals, k, payload=None):
    n_rows = vals.shape[0]
    rio = lax.broadcasted_iota(jnp.int32, vals.shape, 0)
    out_v, out_p = [], []
    for _ in range(k):
        m = jnp.max(vals, axis=0, keepdims=True)
        am = jnp.min(jnp.where(vals == m, rio, n_rows), axis=0, keepdims=True)
        hit = rio == am
        out_v.append(m)
        out_p.append(am if payload is None else jnp.max(jnp.where(hit, payload, -1), axis=0, keepdims=True))
        vals = jnp.where(hit, -jnp.inf, vals)
    return jnp.concatenate(out_v, axis=0), jnp.concatenate(out_p, axis=0), vals


def _topk_rows(vals, k, payload=None):
    return _topk_extract(vals, k, payload)[:2]


_CAND_PER_ROW = [PEER_TOPK // (i + 1) for i in range(PEER_TOPK)]
_N_CAND = sum(_CAND_PER_ROW)
_N_CAND_PAD = -(-_N_CAND // 8) * 8


TOPK_TOKENS = 256


def _subkey_scores(q_ref, keys_ref, h, p):
    half = PEER_DQ // 2
    off = pl.multiple_of(h * PEER_DQ + p * half, half)
    return _dot_nt(keys_ref[p], q_ref[:, pl.ds(off, half)], HI)


def _subkey_top(q_ref, keys_ref, h, p):
    return _topk_rows(_subkey_scores(q_ref, keys_ref, h, p), PEER_TOPK)


def _pair_top(s1, i1, s2, i2, h, cs_ref, ce_ref, idx_t, gate_t):
    tt = s1.shape[1]
    cs_ref[_N_CAND:, :] = jnp.full((_N_CAND_PAD - _N_CAND, tt), -jnp.inf, F32)
    ce_ref[_N_CAND:, :] = jnp.full((_N_CAND_PAD - _N_CAND, tt), -1, jnp.int32)
    row0 = 0
    for i, cnt in enumerate(_CAND_PER_ROW):
        cs_ref[row0:row0 + cnt, :] = s1[i:i + 1] + s2[0:cnt]
        ce_ref[row0:row0 + cnt, :] = i1[i:i + 1] * N_KEYS + i2[0:cnt]
        row0 += cnt
    top_s, top_e = _topk_rows(cs_ref[...], PEER_TOPK, ce_ref[...])
    e = jnp.exp(top_s - top_s[0:1])
    gate = e / jnp.sum(e, axis=0, keepdims=True)
    r0 = pl.multiple_of(h * PEER_TOPK, PEER_TOPK)
    idx_t[pl.ds(r0, PEER_TOPK), :] = top_e
    gate_t[pl.ds(r0, PEER_TOPK), :] = gate


def _topk_scratch(tt):
    return [pltpu.VMEM((NSEL, tt), jnp.int32), pltpu.VMEM((NSEL, tt), F32),
            pltpu.VMEM((_N_CAND_PAD, tt), F32), pltpu.VMEM((_N_CAND_PAD, tt), jnp.int32)]


def _peer_topk_kernel(q_ref, keys_ref, idx_ref, gate_ref, idx_t, gate_t, cs_ref, ce_ref):
    def head(h, carry):
        s1, i1 = _subkey_top(q_ref, keys_ref, h, 0)
        s2, i2 = _subkey_top(q_ref, keys_ref, h, 1)
        _pair_top(s1, i1, s2, i2, h, cs_ref, ce_ref, idx_t, gate_t)
        return carry

    lax.fori_loop(0, PEER_HEADS, head, 0)
    idx_ref[...] = idx_t[...].T
    gate_ref[...] = gate_t[...].T


def peer_topk(qry, sub_keys, n_rows):
    n = n_rows
    tt = _row_tile(n, TOPK_TOKENS)
    nsel = NSEL
    return pl.pallas_call(
        _peer_topk_kernel,
        grid=(n // tt,),
        in_specs=[
            pl.BlockSpec((tt, PEER_HEADS * PEER_DQ), lambda i: (i, 0)),
            pl.BlockSpec((2, N_KEYS, PEER_DQ // 2), lambda i: (0, 0, 0)),
        ],
        out_specs=[
            pl.BlockSpec((tt, nsel), lambda i: (i, 0)),
            pl.BlockSpec((tt, nsel), lambda i: (i, 0)),
        ],
        out_shape=[
            jax.ShapeDtypeStruct((n, nsel), jnp.int32),
            jax.ShapeDtypeStruct((n, nsel), F32),
        ],
        scratch_shapes=_topk_scratch(tt),
        compiler_params=_cparams(("parallel",)),
        name="peer_topk",
    )(qry, sub_keys)


PEER_GROUP = 8
NSEL = PEER_HEADS * PEER_TOPK
SLAB_ROWS = 8
SLAB_PITCH = 12


def peer_table(u, v):
    ne, d = u.shape
    assert d == SLAB_ROWS * LANES
    hi = lax.bitcast_convert_type(u.astype(BF16), jnp.uint16).astype(jnp.uint32)
    lo = lax.bitcast_convert_type(v.astype(BF16), jnp.uint16).astype(jnp.uint32)
    return ((hi << 16) | lo).reshape(ne, SLAB_ROWS, LANES)


SIDE_SLICES = 2 * PEER_HEADS


def _peer_expert_kernel(*refs, final_norm, side):
    if side:
        (idx_ref, idxn_ref, x_ref, gate_ref, g_ref, gf_ref, uv_hbm, qn_ref, keys_ref,
         o_ref, idxo_ref, gateo_ref, gbuf0, gbuf1, sem, idx_t, gate_t, cs_ref, ce_ref, s_keep, i_keep,
         s_half, i_half, rest_ref) = refs
    else:
        idx_ref, idxn_ref, x_ref, gate_ref, g_ref, gf_ref, uv_hbm, o_ref, gbuf0, gbuf1, sem = refs
    g = PEER_GROUP
    d = x_ref.shape[1]
    nsub = d // LANES
    step = pl.program_id(0)
    gbufs = (gbuf0, gbuf1)

    half_k = PEER_TOPK // 2

    def side_first_half():
        k = step % SIDE_SLICES
        sc = _subkey_scores(qn_ref, keys_ref, k // 2, k % 2)
        s, i, rest = _topk_extract(sc, half_k)
        s_half[...] = s
        i_half[...] = i
        rest_ref[...] = rest

    def side_second_half():
        k = step % SIDE_SLICES
        h = k // 2
        p = k % 2
        s, i, _ = _topk_extract(rest_ref[...], half_k)
        s = jnp.concatenate([s_half[...], s], axis=0)
        i = jnp.concatenate([i_half[...], i], axis=0)

        @pl.when(p == 0)
        def _():
            s_keep[...] = s
            i_keep[...] = i

        @pl.when(p == 1)
        def _():
            _pair_top(s_keep[...], i_keep[...], s, i, h, cs_ref, ce_ref, idx_t, gate_t)

        @pl.when(k == SIDE_SLICES - 1)
        def _():
            idxo_ref[...] = idx_t[...].T
            gateo_ref[...] = gate_t[...].T

    def slab_copy(e, slot, t, r):
        row0 = (t * NSEL + r) * SLAB_PITCH
        return pltpu.make_async_copy(uv_hbm.at[e], gbufs[slot].at[pl.ds(row0, SLAB_ROWS), :], sem.at[slot])

    def issue_token(iref, t_src, slot, t):
        for r in range(NSEL):
            slab_copy(iref[t_src, r], slot, t, r).start(priority=r % 2)

    def wait_group(slot):
        def body(t, c):
            for r in range(NSEL):
                slab_copy(0, slot, 0, 0).wait()
            return c
        lax.fori_loop(0, g, body, 0)

    def expert_rows(slot, t):
        base = t * NSEL * SLAB_PITCH
        words = jnp.concatenate(
            [gbufs[slot][pl.ds(base + s, NSEL, stride=SLAB_PITCH), :] for s in range(nsub)], axis=1)
        u = lax.bitcast_convert_type(words & jnp.uint32(0xFFFF0000), F32).astype(BF16)
        v = lax.bitcast_convert_type(words << 16, F32).astype(BF16)
        return u, v

    def combine_group(row0, slot, iref, t_src0, slot_next):
        x = x_ref[row0:row0 + g, :]
        xnb = _rms(x, g_ref[...]).astype(BF16)
        gate = gate_ref[row0:row0 + g, :]
        rio = lax.broadcasted_iota(jnp.int32, (g, NSEL), 0)
        out = x
        for t in range(g):
            issue_token(iref, t_src0 + t, slot_next, t)
            u, v = expert_rows(slot, t)
            act = _dot_nt(xnb, u)
            w = jnp.where(rio == t, gate * jax.nn.gelu(act), 0.0).astype(BF16)
            out = out + _dot(w, v)
        if final_norm:
            out = _rms(out, gf_ref[...])
        o_ref[row0:row0 + g, :] = out

    @pl.when(step == 0)
    def _():
        for t in range(g):
            issue_token(idx_ref, t, 0, t)

    if side:
        side_first_half()
    wait_group(0)
    combine_group(0, 0, idx_ref, g, 1)
    if side:
        side_second_half()
    wait_group(1)
    combine_group(g, 1, idxn_ref, 0, 0)

    @pl.when(step == pl.num_programs(0) - 1)
    def _():
        wait_group(0)


def peer_experts(x, idx, gate, norm_g, final_g, uv, final_norm, row0=0, side=None):
    n = idx.shape[0]
    d = x.shape[1]
    g = PEER_GROUP
    step_rows = 2 * g
    nsteps = n // step_rows
    ngroups = 2 * nsteps
    assert n % step_rows == 0 and row0 % step_rows == 0
    blk0 = row0 // step_rows
    in_specs = [
        pl.BlockSpec((step_rows, NSEL), lambda i: (i, 0), memory_space=pltpu.SMEM),
        pl.BlockSpec((g, NSEL), lambda i: (jnp.minimum(2 * i + 2, ngroups - 1), 0), memory_space=pltpu.SMEM),
        pl.BlockSpec((step_rows, d), lambda i: (i + blk0, 0)),
        pl.BlockSpec((step_rows, NSEL), lambda i: (i, 0)),
        pl.BlockSpec((1, d), lambda i: (0, 0)),
        pl.BlockSpec((1, d), lambda i: (0, 0)),
        pl.BlockSpec(memory_space=pl.ANY),
    ]
    args = [idx, idx, x, gate, norm_g.reshape(1, d), final_g.reshape(1, d), uv]
    out_specs = [pl.BlockSpec((step_rows, d), lambda i: (i + blk0, 0))]
    out_shape = [jax.ShapeDtypeStruct(x.shape, F32)]
    scratch = [
        pltpu.VMEM((g * NSEL * SLAB_PITCH, LANES), jnp.uint32),
        pltpu.VMEM((g * NSEL * SLAB_PITCH, LANES), jnp.uint32),
        pltpu.SemaphoreType.DMA((2,)),
    ]
    if side is not None:
        qry, sub_keys, side_row0 = side
        tt = TOPK_TOKENS
        assert tt == SIDE_SLICES * step_rows and n % tt == 0 and side_row0 % tt == 0
        sblk0 = side_row0 // tt
        in_specs += [
            pl.BlockSpec((tt, qry.shape[1]), lambda i: (i // SIDE_SLICES + sblk0, 0)),
            pl.BlockSpec(sub_keys.shape, lambda i: (0, 0, 0)),
        ]
        args += [qry, sub_keys]
        out_specs += [pl.BlockSpec((tt, NSEL), lambda i: (i // SIDE_SLICES, 0)),
                      pl.BlockSpec((tt, NSEL), lambda i: (i // SIDE_SLICES, 0))]
        out_shape += [jax.ShapeDtypeStruct((n, NSEL), jnp.int32), jax.ShapeDtypeStruct((n, NSEL), F32)]
        scratch += _topk_scratch(tt) + [
            pltpu.VMEM((PEER_TOPK, tt), F32), pltpu.VMEM((PEER_TOPK, tt), jnp.int32),
            pltpu.VMEM((PEER_TOPK // 2, tt), F32), pltpu.VMEM((PEER_TOPK // 2, tt), jnp.int32),
            pltpu.VMEM((N_KEYS, tt), F32)]
    res = pl.pallas_call(
        functools.partial(_peer_expert_kernel, final_norm=final_norm, side=side is not None),
        grid=(nsteps,),
        in_specs=in_specs,
        out_specs=out_specs,
        out_shape=out_shape,
        scratch_shapes=scratch,
        input_output_aliases={2: 0},
        compiler_params=_cparams(("arbitrary",)),
        name="peer_experts",
    )(*args)
    return res if side is not None else res[0]


def _block_diag(w):
    nb, bi, bj = w.shape
    out = jnp.zeros((nb * bi, nb * bj), w.dtype)
    for i in range(nb):
        out = out.at[i * bi:(i + 1) * bi, i * bj:(i + 1) * bj].set(w[i])
    return out


def _prep_weights(p):
    depth = p['norm_mix'].shape[0]
    d = p['norm_final'].shape[0]
    w = {}
    a3, a4 = 3 * A_WIDTH, 4 * A_WIDTH
    w['ab_in'], w['ab_out_a'], w['ab_out_b'], w['ga'], w['gx'] = [], [], [], [], []
    for j in range(p['ab_w_in'].shape[0]):
        wi = p['ab_w_in'][j]
        pad = jnp.zeros((d, LANES - 2 * A_HEADS), wi.dtype)
        cols = [wi[:, :a4], wi[:, a4 + 2 * A_HEADS:], wi[:, a4:a4 + 2 * A_HEADS], pad]
        w['ab_in'].append(jnp.concatenate(cols, axis=1).astype(BF16))
        w['ab_out_a'].append(p['ab_w_out'][j][:A_WIDTH].astype(BF16))
        w['ab_out_b'].append(p['ab_w_out'][j][A_WIDTH:].astype(BF16))
        w['ga'].append(_block_diag(p['b_gate_a_w'][j]).astype(BF16))
        w['gx'].append(_block_diag(p['b_gate_x_w'][j]).astype(BF16))
    w['c_qkv'] = [m.astype(BF16) for m in p['c_w_qkv']]
    w['c_out'] = [m.astype(BF16) for m in p['c_w_out']]
    w['c_lam'] = [jnp.stack([p['c_lambda_q1'][j], p['c_lambda_k1'][j], p['c_lambda_q2'][j], p['c_lambda_k2'][j]])
                  for j in range(p['c_w_qkv'].shape[0])]
    w['peer_q'] = [p['peer_w_query'][l].astype(BF16) for l in range(depth)]
    w['peer_uv'] = [peer_table(p['peer_u'][l], p['peer_v'][l]) for l in range(depth)]
    return w


PEER_PARTS = 16


def _peer(x2, l, p, w, final_norm):
    n = x2.shape[0]
    keys = p['peer_sub_keys'][l]
    qry = norm_matmul(x2, p['norm_ffn'][l], w['peer_q'][l])
    parts = PEER_PARTS if n % (PEER_PARTS * TOPK_TOKENS) == 0 else 1
    n_part = n // parts
    idx, gate = peer_topk(qry, keys, n_part)
    for i in range(parts):
        side = (qry, keys, (i + 1) * n_part) if i + 1 < parts else None
        res = peer_experts(x2, idx, gate, p['norm_ffn'][l], p['norm_final'], w['peer_uv'][l], final_norm,
                           row0=i * n_part, side=side)
        x2, idx, gate = res if side is not None else (res, None, None)
    return x2


def _trunk(x, st_a_conv, st_a_delta, st_b_conv, st_b_h, c_k, c_v, p, w, past):
    b, t, d = x.shape
    n = b * t
    chunk = min(CHUNK, t)
    depth = p['norm_mix'].shape[0]
    x2 = x.reshape(n, d)
    a_conv, a_delta, b_conv, b_h, new_k, new_v = [], [], [], [], [], []
    for l in range(depth):
        j = l // 2
        if l % 2 == 0:
            proj = norm_matmul(x2, p['norm_mix'][l], w['ab_in'][j]).reshape(b, t, -1)
            o_a, s_new = gdn(proj, st_a_conv[j], st_a_delta[j], p['a_conv_w'][j], p['a_log'][j],
                             p['a_dt_bias'][j], p['a_norm'][j], chunk)
            o_b, h_last = lru(proj, st_b_conv[j], st_b_h[j], p['b_conv_w'][j], p['b_conv_b'][j], w['ga'][j],
                              p['b_gate_a_b'][j], w['gx'][j], p['b_gate_x_b'][j], p['b_lambda'][j])
            a3 = 3 * A_WIDTH
            xb0 = a3 + A_WIDTH
            a_conv.append(jnp.concatenate([st_a_conv[j], proj[:, :, :a3]], axis=1)[:, -(CONV_W - 1):])
            b_conv.append(jnp.concatenate([st_b_conv[j], proj[:, :, xb0:xb0 + B_WIDTH]], axis=1)[:, -(CONV_W - 1):])
            a_delta.append(s_new)
            b_h.append(h_last)
            x2 = matmul_res([o_a.reshape(n, A_WIDTH), o_b.reshape(n, B_WIDTH)],
                            [w['ab_out_a'][j], w['ab_out_b'][j]], x2)
        else:
            lam_init = 0.8 - 0.6 * math.exp(-0.3 * l)
            qkv = norm_matmul(x2, p['norm_mix'][l], w['c_qkv'][j]).reshape(b, t, 3 * C_WIDTH)
            kn = qkv[:, :, C_WIDTH:2 * C_WIDTH]
            vn = qkv[:, :, 2 * C_WIDTH:]
            kp = None if c_k is None else c_k[j].reshape(b, -1, C_WIDTH)
            vp = None if c_v is None else c_v[j].reshape(b, -1, C_WIDTH)
            assert past == (0 if kp is None else kp.shape[1])
            o_c = diff_attn(qkv, kp, vp, w['c_lam'][j], p['c_norm'][j], lam_init)
            new_k.append(kn.reshape(b, t, C_HEADS, 2, C_DH))
            new_v.append(vn.reshape(b, t, C_HEADS, C_VD))
            x2 = matmul_res([o_c.reshape(n, C_WIDTH)], [w['c_out'][j]], x2)
        x2 = _peer(x2, l, p, w, final_norm=(l == depth - 1))
    return (x2.reshape(b, t, d), jnp.stack(a_conv), jnp.stack(a_delta), jnp.stack(b_conv), jnp.stack(b_h),
            jnp.stack(new_k), jnp.stack(new_v))


def kernel(x_prompt, x_sample, state_a_conv, state_a_delta, state_b_conv, state_b_rglru, cache_c_k, cache_c_v, norm_mix, norm_ffn, norm_final, ab_w_in, ab_w_out, a_conv_w, a_log, a_dt_bias, a_norm, b_conv_w, b_conv_b, b_gate_a_w, b_gate_a_b, b_gate_x_w, b_gate_x_b, b_lambda, c_w_qkv, c_w_out, c_lambda_q1, c_lambda_k1, c_lambda_q2, c_lambda_k2, c_norm, peer_w_query, peer_sub_keys, peer_u, peer_v):
    p = dict(norm_mix=norm_mix, norm_ffn=norm_ffn, norm_final=norm_final, ab_w_in=ab_w_in,
             ab_w_out=ab_w_out, a_conv_w=a_conv_w, a_log=a_log, a_dt_bias=a_dt_bias, a_norm=a_norm,
             b_conv_w=b_conv_w, b_conv_b=b_conv_b, b_gate_a_w=b_gate_a_w, b_gate_a_b=b_gate_a_b,
             b_gate_x_w=b_gate_x_w, b_gate_x_b=b_gate_x_b, b_lambda=b_lambda, c_w_qkv=c_w_qkv,
             c_w_out=c_w_out, c_lambda_q1=c_lambda_q1, c_lambda_k1=c_lambda_k1,
             c_lambda_q2=c_lambda_q2, c_lambda_k2=c_lambda_k2, c_norm=c_norm,
             peer_w_query=peer_w_query, peer_sub_keys=peer_sub_keys, peer_u=peer_u, peer_v=peer_v)
    w = _prep_weights(p)
    bp = x_prompt.shape[0]
    dt = x_prompt.dtype
    n_even = state_a_conv.shape[0]
    z_a_conv = jnp.zeros((n_even, bp) + state_a_conv.shape[2:], dt)
    z_a_delta = jnp.zeros((n_even, bp) + state_a_delta.shape[2:], dt)
    z_b_conv = jnp.zeros((n_even, bp) + state_b_conv.shape[2:], dt)
    z_b_h = jnp.zeros((n_even, bp) + state_b_rglru.shape[2:], dt)
    (y_prompt, pa_conv, pa_delta, pb_conv, pb_h, pc_k, pc_v) = _trunk(
        x_prompt, z_a_conv, z_a_delta, z_b_conv, z_b_h, None, None, p, w, 0)
    (y_sample, sa_conv, sa_delta, sb_conv, sb_h, sc_k, sc_v) = _trunk(
        x_sample, state_a_conv, state_a_delta, state_b_conv, state_b_rglru, cache_c_k, cache_c_v,
        p, w, cache_c_k.shape[2])
    return (y_prompt, y_sample, pa_conv, pa_delta, pb_conv, pb_h, pc_k, pc_v,
            sa_conv, sa_delta, sb_conv, sb_h, sc_k, sc_v)
```

```python
import functools
import math

import jax
import jax.numpy as jnp
from jax import lax
from jax.experimental import pallas as pl
from jax.experimental.pallas import tpu as pltpu

F32 = jnp.float32
BF16 = jnp.bfloat16
HI = lax.Precision.HIGHEST

EPS = 1e-6
CHUNK = 64
CONV_W = 4
A_HEADS = 4
A_DK = 128
A_WIDTH = A_HEADS * A_DK
B_WIDTH = 512
RG_C = 8.0
C_HEADS = 8
C_DH = 64
C_VD = 2 * C_DH
C_WIDTH = C_HEADS * C_VD
PEER_HEADS = 8
N_KEYS = 128
PEER_TOPK = 16
PEER_DQ = 256
LANES = 128
VMEM_LIMIT = 48 * 1024 * 1024


def _cparams(sem):
    return pltpu.CompilerParams(dimension_semantics=sem, vmem_limit_bytes=VMEM_LIMIT)


def _dot(a, b, prec=None):
    return jnp.dot(a, b, precision=prec, preferred_element_type=F32)


def _dot_nt(a, b, prec=None):
    return lax.dot_general(a, b, (((1,), (1,)), ((), ())), precision=prec, preferred_element_type=F32)


def _dot_bf16(a, b):
    return _dot(a.astype(BF16), b.astype(BF16))


def _dot_sel(sel, b, nt=False):
    f = _dot_nt if nt else _dot
    s = sel.astype(F32).astype(BF16)
    b1 = b.astype(BF16)
    r1 = b - b1.astype(F32)
    b2 = r1.astype(BF16)
    b3 = (r1 - b2.astype(F32)).astype(BF16)
    return f(s, b1) + f(s, b2) + f(s, b3)


def _rms(x, g):
    return x * lax.rsqrt(jnp.mean(x * x, axis=-1, keepdims=True) + EPS) * g


def _softplus(x):
    return jnp.maximum(x, 0.0) + jnp.log1p(jnp.exp(-jnp.abs(x)))


def _row_tile(n, pref):
    t = min(n, pref)
    while n % t:
        t //= 2
    return t


def _norm_matmul_kernel(x_ref, g_ref, w_ref, o_ref):
    y = _rms(x_ref[...], g_ref[...])
    o_ref[...] = _dot(y.astype(BF16), w_ref[...])


def norm_matmul(x, g, w):
    n, d = x.shape
    m = w.shape[1]
    tm = _row_tile(n, 256)
    return pl.pallas_call(
        _norm_matmul_kernel,
        grid=(n // tm,),
        in_specs=[
            pl.BlockSpec((tm, d), lambda i: (i, 0)),
            pl.BlockSpec((1, d), lambda i: (0, 0)),
            pl.BlockSpec((d, m), lambda i: (0, 0)),
        ],
        out_specs=pl.BlockSpec((tm, m), lambda i: (i, 0)),
        out_shape=jax.ShapeDtypeStruct((n, m), F32),
        compiler_params=_cparams(("parallel",)),
        name="norm_matmul",
    )(x, g.reshape(1, d), w)


def _matmul_res_kernel(*refs, n_in):
    ys = refs[:n_in]
    ws = refs[n_in:2 * n_in]
    res_ref = refs[2 * n_in]
    o_ref = refs[2 * n_in + 1]
    acc = res_ref[...]
    for y_ref, w_ref in zip(ys, ws):
        acc = acc + _dot(y_ref[...].astype(BF16), w_ref[...])
    o_ref[...] = acc


def matmul_res(ys, ws, res):
    n, d = res.shape
    tm = _row_tile(n, 512)
    n_in = len(ys)
    in_specs = [pl.BlockSpec((tm, y.shape[1]), lambda i: (i, 0)) for y in ys]
    in_specs += [pl.BlockSpec(w.shape, lambda i: (0, 0)) for w in ws]
    in_specs += [pl.BlockSpec((tm, d), lambda i: (i, 0))]
    return pl.pallas_call(
        functools.partial(_matmul_res_kernel, n_in=n_in),
        grid=(n // tm,),
        in_specs=in_specs,
        out_specs=pl.BlockSpec((tm, d), lambda i: (i, 0)),
        out_shape=jax.ShapeDtypeStruct((n, d), F32),
        compiler_params=_cparams(("parallel",)),
        name="matmul_res",
    )(*ys, *ws, res)


def _gdn_kernel(qkv_ref, z_ref, ba_ref, cprev_ref, s0_ref, cw_ref, prm_ref, an_ref,
                o_ref, sout_ref, xp_ref, s_ref, *, chunk):
    c = chunk
    nh = A_HEADS
    r = nh * c
    step = pl.program_id(1)

    @pl.when(step == 0)
    def _():
        xp_ref[5:8, :] = cprev_ref[0]
        for h in range(nh):
            s_ref[:, h * A_DK:(h + 1) * A_DK] = s0_ref[0, h]

    xp_ref[8:8 + c, :] = qkv_ref[0]
    conv = xp_ref[5:5 + c, :] * cw_ref[0:1, :]
    for j in range(1, CONV_W):
        conv = conv + xp_ref[5 + j:5 + j + c, :] * cw_ref[j:j + 1, :]
    xp_ref[5:8, :] = xp_ref[5 + c:8 + c, :]
    act = conv * jax.nn.sigmoid(conv)

    def stack(off):
        return jnp.concatenate([act[:, off + h * A_DK: off + (h + 1) * A_DK] for h in range(nh)], axis=0)

    qs, ks, vs = stack(0), stack(A_WIDTH), stack(2 * A_WIDTH)
    qs = qs * lax.rsqrt(jnp.sum(qs * qs, axis=-1, keepdims=True) + EPS) * (A_DK ** -0.5)
    ks = ks * lax.rsqrt(jnp.sum(ks * ks, axis=-1, keepdims=True) + EPS)

    ba = ba_ref[0]
    beta_full = jax.nn.sigmoid(ba)
    g_full = -jnp.exp(prm_ref[0:1, :]) * _softplus(ba + prm_ref[1:2, :])
    beta_s = jnp.concatenate([beta_full[:, h:h + 1] for h in range(nh)], axis=0)
    g_s = jnp.concatenate([g_full[:, nh + h:nh + h + 1] for h in range(nh)], axis=0)

    row = lax.broadcasted_iota(jnp.int32, (r, r), 0)
    col = lax.broadcasted_iota(jnp.int32, (r, r), 1)
    same = (row // c) == (col // c)
    m_incl = same & (col <= row)
    m_strict = same & (col < row)

    gcs = _dot_sel(m_incl, jnp.broadcast_to(g_s, (r, LANES)))
    lane0 = lax.broadcasted_iota(jnp.int32, (r, LANES), 1) == 0
    gcs_row = _dot_sel(lane0, gcs, nt=True)
    gc = gcs[:, 0:1]
    decay = jnp.exp(jnp.where(m_incl, gc - gcs_row, -jnp.inf))

    kb = ks * beta_s
    ks_b = ks.astype(BF16)
    a_mat = jnp.where(m_strict, _dot_nt(kb.astype(BF16), ks_b), 0.0) * decay
    eye = (row == col).astype(F32)
    p = -a_mat
    t_inv = eye + p
    for _ in range(int(math.log2(c)) - 1):
        p = _dot_bf16(p, p)
        t_inv = t_inv + _dot_bf16(t_inv, p)
    uw = _dot_bf16(t_inv, jnp.concatenate([vs * beta_s, kb * jnp.exp(gc)], axis=1))
    u, w = uw[:, :A_DK], uw[:, A_DK:]
    qk = jnp.where(m_incl, _dot_nt(qs.astype(BF16), ks_b), 0.0) * decay

    def diag_blocks(x):
        return jnp.concatenate([x[h * c:(h + 1) * c, h * A_DK:(h + 1) * A_DK] for h in range(nh)], axis=0)

    s_cat = s_ref[...]
    wq_s = _dot_bf16(jnp.concatenate([w, qs * jnp.exp(gc)], axis=0), s_cat)
    v_new = u - diag_blocks(wq_s[:r])
    o_s = diag_blocks(wq_s[r:]) + _dot_bf16(qk, v_new)

    g_last = jnp.concatenate(
        [jnp.broadcast_to(gcs[(h + 1) * c - 1:(h + 1) * c, :], (c, LANES)) for h in range(nh)], axis=0)
    kdec = ks * jnp.exp(g_last - gcs)
    zero = jnp.zeros((c, A_DK), F32)
    v_exp = jnp.concatenate(
        [jnp.concatenate([v_new[h * c:(h + 1) * c] if hh == h else zero for hh in range(nh)], axis=1)
         for h in range(nh)], axis=0)
    s_scale = jnp.concatenate(
        [jnp.exp(jnp.broadcast_to(gcs[(h + 1) * c - 1:(h + 1) * c, :], (A_DK, LANES))) for h in range(nh)], axis=1)
    s_new = s_cat * s_scale + _dot_bf16(kdec.T, v_exp)
    s_ref[...] = s_new

    o_n = _rms(o_s, an_ref[...])
    o_full = jnp.concatenate([o_n[h * c:(h + 1) * c] for h in range(nh)], axis=1)
    zz = z_ref[0]
    o_ref[0] = o_full * (zz * jax.nn.sigmoid(zz))

    @pl.when(step == pl.num_programs(1) - 1)
    def _():
        for h in range(nh):
            sout_ref[0, h] = s_new[:, h * A_DK:(h + 1) * A_DK]


def gdn(proj, conv_prev, s0, conv_w, a_log, dt_bias, a_norm, chunk):
    b, t, _ = proj.shape
    nsteps = t // chunk
    prm = jnp.zeros((2, LANES), F32)
    prm = prm.at[0, A_HEADS:2 * A_HEADS].set(a_log).at[1, A_HEADS:2 * A_HEADS].set(dt_bias)
    qkv_w = 3 * A_WIDTH
    return pl.pallas_call(
        functools.partial(_gdn_kernel, chunk=chunk),
        grid=(b, nsteps),
        in_specs=[
            pl.BlockSpec((1, chunk, qkv_w), lambda i, j: (i, j, 0)),
            pl.BlockSpec((1, chunk, A_WIDTH), lambda i, j: (i, j, qkv_w // A_WIDTH)),
            pl.BlockSpec((1, chunk, LANES), lambda i, j: (i, j, (qkv_w + 3 * A_WIDTH) // LANES)),
            pl.BlockSpec((1, CONV_W - 1, qkv_w), lambda i, j: (i, 0, 0)),
            pl.BlockSpec((1, A_HEADS, A_DK, A_DK), lambda i, j: (i, 0, 0, 0)),
            pl.BlockSpec((CONV_W, qkv_w), lambda i, j: (0, 0)),
            pl.BlockSpec((2, LANES), lambda i, j: (0, 0)),
            pl.BlockSpec((1, A_DK), lambda i, j: (0, 0)),
        ],
        out_specs=[
            pl.BlockSpec((1, chunk, A_WIDTH), lambda i, j: (i, j, 0)),
            pl.BlockSpec((1, A_HEADS, A_DK, A_DK), lambda i, j: (i, 0, 0, 0)),
        ],
        out_shape=[
            jax.ShapeDtypeStruct((b, t, A_WIDTH), F32),
            jax.ShapeDtypeStruct((b, A_HEADS, A_DK, A_DK), F32),
        ],
        scratch_shapes=[
            pltpu.VMEM((chunk + 8, qkv_w), F32),
            pltpu.VMEM((A_DK, A_HEADS * A_DK), F32),
        ],
        compiler_params=_cparams(("parallel", "arbitrary")),
        name="gdn",
    )(proj, proj, proj, conv_prev, s0, conv_w, prm, a_norm.reshape(1, A_DK))


def _lru_kernel(xb_ref, gb_ref, cprev_ref, h0_ref, cw_ref, cb_ref, wa_ref, ba_ref, wx_ref, bx_ref, lam_ref,
                ob_ref, hl_ref, xp_ref, a_ref, u_ref, hs_ref, h_ref, *, tb):
    step = pl.program_id(1)

    @pl.when(step == 0)
    def _():
        xp_ref[5:8, :] = cprev_ref[0]
        h_ref[...] = h0_ref[0]

    xp_ref[8:8 + tb, :] = xb_ref[0]
    xc = xp_ref[5:5 + tb, :] * cw_ref[0:1, :]
    for j in range(1, CONV_W):
        xc = xc + xp_ref[5 + j:5 + j + tb, :] * cw_ref[j:j + 1, :]
    xp_ref[5:8, :] = xp_ref[5 + tb:8 + tb, :]
    xc = xc + cb_ref[...]
    xcb = xc.astype(BF16)
    rg = jax.nn.sigmoid(_dot(xcb, wa_ref[...]) + ba_ref[...])
    ig = jax.nn.sigmoid(_dot(xcb, wx_ref[...]) + bx_ref[...])
    log_a = -RG_C * rg * _softplus(-lam_ref[...])
    a_ref[...] = jnp.exp(log_a)
    u_ref[...] = jnp.sqrt(1.0 - jnp.exp(2.0 * log_a)) * (ig * xc)

    def body(i, h):
        h = a_ref[pl.ds(i, 1), :] * h + u_ref[pl.ds(i, 1), :]
        hs_ref[pl.ds(i, 1), :] = h
        return h

    h = lax.fori_loop(0, tb, body, h_ref[...], unroll=8)
    h_ref[...] = h
    ob_ref[0] = hs_ref[...] * jax.nn.gelu(gb_ref[0])

    @pl.when(step == pl.num_programs(1) - 1)
    def _():
        hl_ref[0] = h


def lru(proj, conv_prev, h0, conv_w, conv_b, wa_bd, ba, wx_bd, bx, lam):
    b, t, _ = proj.shape
    tb = _row_tile(t, 256)
    xb_blk = (3 * A_WIDTH + A_WIDTH) // B_WIDTH
    vec = lambda v: v.reshape(1, B_WIDTH)
    const = lambda shape: pl.BlockSpec(shape, lambda i, j: (0,) * len(shape))
    ob, hl = pl.pallas_call(
        functools.partial(_lru_kernel, tb=tb),
        grid=(b, t // tb),
        in_specs=[
            pl.BlockSpec((1, tb, B_WIDTH), lambda i, j: (i, j, xb_blk)),
            pl.BlockSpec((1, tb, B_WIDTH), lambda i, j: (i, j, xb_blk + 1)),
            pl.BlockSpec((1, CONV_W - 1, B_WIDTH), lambda i, j: (i, 0, 0)),
            pl.BlockSpec((1, 1, B_WIDTH), lambda i, j: (i, 0, 0)),
            const((CONV_W, B_WIDTH)), const((1, B_WIDTH)),
            const((B_WIDTH, B_WIDTH)), const((1, B_WIDTH)),
            const((B_WIDTH, B_WIDTH)), const((1, B_WIDTH)),
            const((1, B_WIDTH)),
        ],
        out_specs=[
            pl.BlockSpec((1, tb, B_WIDTH), lambda i, j: (i, j, 0)),
            pl.BlockSpec((1, 1, B_WIDTH), lambda i, j: (i, 0, 0)),
        ],
        out_shape=[
            jax.ShapeDtypeStruct((b, t, B_WIDTH), F32),
            jax.ShapeDtypeStruct((b, 1, B_WIDTH), F32),
        ],
        scratch_shapes=[
            pltpu.VMEM((tb + 8, B_WIDTH), F32),
            pltpu.VMEM((tb, B_WIDTH), F32),
            pltpu.VMEM((tb, B_WIDTH), F32),
            pltpu.VMEM((tb, B_WIDTH), F32),
            pltpu.VMEM((1, B_WIDTH), F32),
        ],
        compiler_params=_cparams(("parallel", "arbitrary")),
        name="lru",
    )(proj, proj, conv_prev, h0.reshape(b, 1, B_WIDTH), conv_w, vec(conv_b), wa_bd, vec(ba), wx_bd, vec(bx), vec(lam))
    return ob, hl.reshape(b, B_WIDTH)


def _attn_kernel(q_ref, kp_ref, vp_ref, kn_ref, vn_ref, lam_ref, ng_ref, o_ref, *, tq, nq, past, lam_init):
    qi = pl.program_id(2)
    q = q_ref[0] * (C_DH ** -0.5)
    lane = lax.broadcasted_iota(jnp.int32, q.shape, 1)
    qc = (jnp.where(lane < C_DH, q, 0.0).astype(BF16), jnp.where(lane >= C_DH, q, 0.0).astype(BF16))
    lv = lam_ref[...]
    lam = (jnp.exp(jnp.sum(lv[0:1] * lv[1:2], axis=-1, keepdims=True))
           - jnp.exp(jnp.sum(lv[2:3] * lv[3:4], axis=-1, keepdims=True)) + lam_init)
    rpos = lax.broadcasted_iota(jnp.int32, (tq, tq), 0)
    cpos = lax.broadcasted_iota(jnp.int32, (tq, tq), 1)
    tail_mask = (cpos // CHUNK) <= (rpos // CHUNK)

    def tile(i):
        p0 = past + i * tq
        kt = kn_ref[0, i * tq:(i + 1) * tq, :].astype(BF16)
        vt = vn_ref[0, i * tq:(i + 1) * tq, :].astype(BF16)
        if p0:
            kf = kp_ref[0, 0:p0, :].astype(BF16)
            vf = vp_ref[0, 0:p0, :].astype(BF16)
        o = None
        for comp in range(2):
            st = jnp.where(tail_mask, _dot_nt(qc[comp], kt), -jnp.inf)
            m = jnp.max(st, axis=-1, keepdims=True)
            if p0:
                sf = _dot_nt(qc[comp], kf)
                m = jnp.maximum(m, jnp.max(sf, axis=-1, keepdims=True))
            et = jnp.exp(st - m)
            l = jnp.sum(et, axis=-1, keepdims=True)
            acc = _dot(et.astype(BF16), vt)
            if p0:
                ef = jnp.exp(sf - m)
                l = l + jnp.sum(ef, axis=-1, keepdims=True)
                acc = acc + _dot(ef.astype(BF16), vf)
            oc = acc / l
            o = oc if comp == 0 else o - lam * oc
        o_ref[0] = _rms(o, ng_ref[...]) * (1.0 - lam_init)

    for i in range(nq):
        pl.when(qi == i)(functools.partial(tile, i))


def diff_attn(qkv, k_past, v_past, lam_vecs, norm_g, lam_init):
    b, t_q, _ = qkv.shape
    past = 0 if k_past is None else k_past.shape[1]
    tq = _row_tile(t_q, 256)
    nq = t_q // tq
    assert past % CHUNK == 0 and (nq == 1 or (tq % CHUNK == 0 and past == 0))
    new_blk = lambda col0: pl.BlockSpec((1, t_q, C_VD), lambda i, h, j: (i, 0, col0 + h))
    if k_past is None:
        pre, pre_specs = (qkv, qkv), [new_blk(C_HEADS), new_blk(2 * C_HEADS)]
    else:
        pre = (k_past, v_past)
        pre_specs = [pl.BlockSpec((1, past, C_VD), lambda i, h, j: (i, 0, h)) for _ in range(2)]
    return pl.pallas_call(
        functools.partial(_attn_kernel, tq=tq, nq=nq, past=past, lam_init=lam_init),
        grid=(b, C_HEADS, nq),
        in_specs=[pl.BlockSpec((1, tq, C_VD), lambda i, h, j: (i, j, h))] + pre_specs + [
            new_blk(C_HEADS), new_blk(2 * C_HEADS),
            pl.BlockSpec((4, C_DH), lambda i, h, j: (0, 0)),
            pl.BlockSpec((1, C_VD), lambda i, h, j: (0, 0)),
        ],
        out_specs=pl.BlockSpec((1, tq, C_VD), lambda i, h, j: (i, j, h)),
        out_shape=jax.ShapeDtypeStruct((b, t_q, C_WIDTH), F32),
        compiler_params=_cparams(("parallel", "parallel", "arbitrary")),
        name="diff_attn",
    )(qkv, *pre, qkv, qkv, lam_vecs, norm_g.reshape(1, C_VD))


def _topk_extract(vals, k, payload=None):
    n_rows = vals.shape[0]
    rio = lax.broadcasted_iota(jnp.int32, vals.shape, 0)
    out_v, out_p = [], []
    for _ in range(k):
        m = jnp.max(vals, axis=0, keepdims=True)
        am = jnp.min(jnp.where(vals == m, rio, n_rows), axis=0, keepdims=True)
        hit = rio == am
        out_v.append(m)
        out_p.append(am if payload is None else jnp.max(jnp.where(hit, payload, -1), axis=0, keepdims=True))
        vals = jnp.where(hit, -jnp.inf, vals)
    return jnp.concatenate(out_v, axis=0), jnp.concatenate(out_p, axis=0), vals


def _topk_rows(vals, k, payload=None):
    return _topk_extract(vals, k, payload)[:2]


_CAND_PER_ROW = [PEER_TOPK // (i + 1) for i in range(PEER_TOPK)]
_N_CAND = sum(_CAND_PER_ROW)
_N_CAND_PAD = -(-_N_CAND // 8) * 8


TOPK_TOKENS = 256


def _subkey_scores(q_ref, keys_ref, h, p):
    half = PEER_DQ // 2
    off = pl.multiple_of(h * PEER_DQ + p * half, half)
    return _dot_nt(keys_ref[p], q_ref[:, pl.ds(off, half)], HI)


def _subkey_top(q_ref, keys_ref, h, p):
    return _topk_rows(_subkey_scores(q_ref, keys_ref, h, p), PEER_TOPK)


def _pair_top(s1, i1, s2, i2, h, cs_ref, ce_ref, idx_t, gate_t):
    tt = s1.shape[1]
    cs_ref[_N_CAND:, :] = jnp.full((_N_CAND_PAD - _N_CAND, tt), -jnp.inf, F32)
    ce_ref[_N_CAND:, :] = jnp.full((_N_CAND_PAD - _N_CAND, tt), -1, jnp.int32)
    row0 = 0
    for i, cnt in enumerate(_CAND_PER_ROW):
        cs_ref[row0:row0 + cnt, :] = s1[i:i + 1] + s2[0:cnt]
        ce_ref[row0:row0 + cnt, :] = i1[i:i + 1] * N_KEYS + i2[0:cnt]
        row0 += cnt
    top_s, top_e = _topk_rows(cs_ref[...], PEER_TOPK, ce_ref[...])
    e = jnp.exp(top_s - top_s[0:1])
    gate = e / jnp.sum(e, axis=0, keepdims=True)
    r0 = pl.multiple_of(h * PEER_TOPK, PEER_TOPK)
    idx_t[pl.ds(r0, PEER_TOPK), :] = top_e
    gate_t[pl.ds(r0, PEER_TOPK), :] = gate


def _topk_scratch(tt):
    return [pltpu.VMEM((NSEL, tt), jnp.int32), pltpu.VMEM((NSEL, tt), F32),
            pltpu.VMEM((_N_CAND_PAD, tt), F32), pltpu.VMEM((_N_CAND_PAD, tt), jnp.int32)]


def _peer_topk_kernel(q_ref, keys_ref, idx_ref, gate_ref, idx_t, gate_t, cs_ref, ce_ref):
    def head(h, carry):
        s1, i1 = _subkey_top(q_ref, keys_ref, h, 0)
        s2, i2 = _subkey_top(q_ref, keys_ref, h, 1)
        _pair_top(s1, i1, s2, i2, h, cs_ref, ce_ref, idx_t, gate_t)
        return carry

    lax.fori_loop(0, PEER_HEADS, head, 0)
    idx_ref[...] = idx_t[...].T
    gate_ref[...] = gate_t[...].T


def peer_topk(qry, sub_keys, n_rows):
    n = n_rows
    tt = _row_tile(n, TOPK_TOKENS)
    nsel = NSEL
    return pl.pallas_call(
        _peer_topk_kernel,
        grid=(n // tt,),
        in_specs=[
            pl.BlockSpec((tt, PEER_HEADS * PEER_DQ), lambda i: (i, 0)),
            pl.BlockSpec((2, N_KEYS, PEER_DQ // 2), lambda i: (0, 0, 0)),
        ],
        out_specs=[
            pl.BlockSpec((tt, nsel), lambda i: (i, 0)),
            pl.BlockSpec((tt, nsel), lambda i: (i, 0)),
        ],
        out_shape=[
            jax.ShapeDtypeStruct((n, nsel), jnp.int32),
            jax.ShapeDtypeStruct((n, nsel), F32),
        ],
        scratch_shapes=_topk_scratch(tt),
        compiler_params=_cparams(("parallel",)),
        name="peer_topk",
    )(qry, sub_keys)


PEER_GROUP = 8
PEER_PRE_TOKENS = 2
NSEL = PEER_HEADS * PEER_TOPK
SLAB_ROWS = 8
SLAB_PITCH = 12


def peer_table(u, v):
    ne, d = u.shape
    assert d == SLAB_ROWS * LANES
    hi = lax.bitcast_convert_type(u.astype(BF16), jnp.uint16).astype(jnp.uint32)
    lo = lax.bitcast_convert_type(v.astype(BF16), jnp.uint16).astype(jnp.uint32)
    return ((hi << 16) | lo).reshape(ne, SLAB_ROWS, LANES)


SIDE_SLICES = 2 * PEER_HEADS


def _peer_expert_kernel(*refs, final_norm, side):
    if side:
        (idx_ref, idxn_ref, x_ref, gate_ref, g_ref, gf_ref, uv_hbm, qn_ref, keys_ref,
         o_ref, idxo_ref, gateo_ref, gbuf0, gbuf1, sem, idx_t, gate_t, cs_ref, ce_ref, s_keep, i_keep,
         s_half, i_half, rest_ref) = refs
    else:
        idx_ref, idxn_ref, x_ref, gate_ref, g_ref, gf_ref, uv_hbm, o_ref, gbuf0, gbuf1, sem = refs
    g = PEER_GROUP
    d = x_ref.shape[1]
    nsub = d // LANES
    step = pl.program_id(0)
    gbufs = (gbuf0, gbuf1)

    half_k = PEER_TOPK // 2

    def side_first_half():
        k = step % SIDE_SLICES
        sc = _subkey_scores(qn_ref, keys_ref, k // 2, k % 2)
        s, i, rest = _topk_extract(sc, half_k)
        s_half[...] = s
        i_half[...] = i
        rest_ref[...] = rest

    def side_second_half():
        k = step % SIDE_SLICES
        h = k // 2
        p = k % 2
        s, i, _ = _topk_extract(rest_ref[...], half_k)
        s = jnp.concatenate([s_half[...], s], axis=0)
        i = jnp.concatenate([i_half[...], i], axis=0)

        @pl.when(p == 0)
        def _():
            s_keep[...] = s
            i_keep[...] = i

        @pl.when(p == 1)
        def _():
            _pair_top(s_keep[...], i_keep[...], s, i, h, cs_ref, ce_ref, idx_t, gate_t)

        @pl.when(k == SIDE_SLICES - 1)
        def _():
            idxo_ref[...] = idx_t[...].T
            gateo_ref[...] = gate_t[...].T

    def slab_copy(e, slot, t, r):
        row0 = (t * NSEL + r) * SLAB_PITCH
        return pltpu.make_async_copy(uv_hbm.at[e], gbufs[slot].at[pl.ds(row0, SLAB_ROWS), :], sem.at[slot])

    def issue_token(iref, t_src, slot, t):
        for r in range(NSEL):
            slab_copy(iref[t_src, r], slot, t, r).start(priority=r % 2)

    def wait_group(slot):
        def body(t, c):
            for r in range(NSEL):
                slab_copy(0, slot, 0, 0).wait()
            return c
        lax.fori_loop(0, g, body, 0)

    def expert_rows(slot, t):
        base = t * NSEL * SLAB_PITCH
        words = jnp.concatenate(
            [gbufs[slot][pl.ds(base + s, NSEL, stride=SLAB_PITCH), :] for s in range(nsub)], axis=1)
        u = lax.bitcast_convert_type(words & jnp.uint32(0xFFFF0000), F32).astype(BF16)
        v = lax.bitcast_convert_type(words << 16, F32).astype(BF16)
        return u, v

    n_pre = PEER_PRE_TOKENS

    def pre_issue(iref, t_src0, slot_next):
        for t in range(n_pre):
            issue_token(iref, t_src0 + t, slot_next, t)

    def combine_group(row0, slot, iref, t_src0, slot_next):
        x = x_ref[row0:row0 + g, :]
        xnb = _rms(x, g_ref[...]).astype(BF16)
        gate = gate_ref[row0:row0 + g, :]
        rio = lax.broadcasted_iota(jnp.int32, (g, NSEL), 0)
        out = x
        for t in range(g):
            if t >= n_pre:
                issue_token(iref, t_src0 + t, slot_next, t)
            u, v = expert_rows(slot, t)
            act = _dot_nt(xnb, u)
            w = jnp.where(rio == t, gate * jax.nn.gelu(act), 0.0).astype(BF16)
            out = out + _dot(w, v)
        if final_norm:
            out = _rms(out, gf_ref[...])
        o_ref[row0:row0 + g, :] = out

    @pl.when(step == 0)
    def _():
        for t in range(g):
            issue_token(idx_ref, t, 0, t)

    pre_issue(idx_ref, g, 1)
    if side:
        side_first_half()
    wait_group(0)
    combine_group(0, 0, idx_ref, g, 1)
    pre_issue(idxn_ref, 0, 0)
    if side:
        side_second_half()
    wait_group(1)
    combine_group(g, 1, idxn_ref, 0, 0)

    @pl.when(step == pl.num_programs(0) - 1)
    def _():
        wait_group(0)


def peer_experts(x, idx, gate, norm_g, final_g, uv, final_norm, row0=0, side=None):
    n = idx.shape[0]
    d = x.shape[1]
    g = PEER_GROUP
    step_rows = 2 * g
    nsteps = n // step_rows
    ngroups = 2 * nsteps
    assert n % step_rows == 0 and row0 % step_rows == 0
    blk0 = row0 // step_rows
    in_specs = [
        pl.BlockSpec((step_rows, NSEL), lambda i: (i, 0), memory_space=pltpu.SMEM),
        pl.BlockSpec((g, NSEL), lambda i: (jnp.minimum(2 * i + 2, ngroups - 1), 0), memory_space=pltpu.SMEM),
        pl.BlockSpec((step_rows, d), lambda i: (i + blk0, 0)),
        pl.BlockSpec((step_rows, NSEL), lambda i: (i, 0)),
        pl.BlockSpec((1, d), lambda i: (0, 0)),
        pl.BlockSpec((1, d), lambda i: (0, 0)),
        pl.BlockSpec(memory_space=pl.ANY),
    ]
    args = [idx, idx, x, gate, norm_g.reshape(1, d), final_g.reshape(1, d), uv]
    out_specs = [pl.BlockSpec((step_rows, d), lambda i: (i + blk0, 0))]
    out_shape = [jax.ShapeDtypeStruct(x.shape, F32)]
    scratch = [
        pltpu.VMEM((g * NSEL * SLAB_PITCH, LANES), jnp.uint32),
        pltpu.VMEM((g * NSEL * SLAB_PITCH, LANES), jnp.uint32),
        pltpu.SemaphoreType.DMA((2,)),
    ]
    if side is not None:
        qry, sub_keys, side_row0 = side
        tt = TOPK_TOKENS
        assert tt == SIDE_SLICES * step_rows and n % tt == 0 and side_row0 % tt == 0
        sblk0 = side_row0 // tt
        in_specs += [
            pl.BlockSpec((tt, qry.shape[1]), lambda i: (i // SIDE_SLICES + sblk0, 0)),
            pl.BlockSpec(sub_keys.shape, lambda i: (0, 0, 0)),
        ]
        args += [qry, sub_keys]
        out_specs += [pl.BlockSpec((tt, NSEL), lambda i: (i // SIDE_SLICES, 0)),
                      pl.BlockSpec((tt, NSEL), lambda i: (i // SIDE_SLICES, 0))]
        out_shape += [jax.ShapeDtypeStruct((n, NSEL), jnp.int32), jax.ShapeDtypeStruct((n, NSEL), F32)]
        scratch += _topk_scratch(tt) + [
            pltpu.VMEM((PEER_TOPK, tt), F32), pltpu.VMEM((PEER_TOPK, tt), jnp.int32),
            pltpu.VMEM((PEER_TOPK // 2, tt), F32), pltpu.VMEM((PEER_TOPK // 2, tt), jnp.int32),
            pltpu.VMEM((N_KEYS, tt), F32)]
    res = pl.pallas_call(
        functools.partial(_peer_expert_kernel, final_norm=final_norm, side=side is not None),
        grid=(nsteps,),
        in_specs=in_specs,
        out_specs=out_specs,
        out_shape=out_shape,
        scratch_shapes=scratch,
        input_output_aliases={2: 0},
        compiler_params=_cparams(("arbitrary",)),
        name="peer_experts",
    )(*args)
    return res if side is not None else res[0]


def _block_diag(w):
    nb, bi, bj = w.shape
    out = jnp.zeros((nb * bi, nb * bj), w.dtype)
    for i in range(nb):
        out = out.at[i * bi:(i + 1) * bi, i * bj:(i + 1) * bj].set(w[i])
    return out


def _prep_weights(p):
    depth = p['norm_mix'].shape[0]
    d = p['norm_final'].shape[0]
    w = {}
    a3, a4 = 3 * A_WIDTH, 4 * A_WIDTH
    w['ab_in'], w['ab_out_a'], w['ab_out_b'], w['ga'], w['gx'] = [], [], [], [], []
    for j in range(p['ab_w_in'].shape[0]):
        wi = p['ab_w_in'][j]
        pad = jnp.zeros((d, LANES - 2 * A_HEADS), wi.dtype)
        cols = [wi[:, :a4], wi[:, a4 + 2 * A_HEADS:], wi[:, a4:a4 + 2 * A_HEADS], pad]
        w['ab_in'].append(jnp.concatenate(cols, axis=1).astype(BF16))
        w['ab_out_a'].append(p['ab_w_out'][j][:A_WIDTH].astype(BF16))
        w['ab_out_b'].append(p['ab_w_out'][j][A_WIDTH:].astype(BF16))
        w['ga'].append(_block_diag(p['b_gate_a_w'][j]).astype(BF16))
        w['gx'].append(_block_diag(p['b_gate_x_w'][j]).astype(BF16))
    w['c_qkv'] = [m.astype(BF16) for m in p['c_w_qkv']]
    w['c_out'] = [m.astype(BF16) for m in p['c_w_out']]
    w['c_lam'] = [jnp.stack([p['c_lambda_q1'][j], p['c_lambda_k1'][j], p['c_lambda_q2'][j], p['c_lambda_k2'][j]])
                  for j in range(p['c_w_qkv'].shape[0])]
    w['peer_q'] = [p['peer_w_query'][l].astype(BF16) for l in range(depth)]
    w['peer_uv'] = [peer_table(p['peer_u'][l], p['peer_v'][l]) for l in range(depth)]
    return w


PEER_PARTS = 16


def _peer(x2, l, p, w, final_norm):
    n = x2.shape[0]
    keys = p['peer_sub_keys'][l]
    qry = norm_matmul(x2, p['norm_ffn'][l], w['peer_q'][l])
    parts = PEER_PARTS if n % (PEER_PARTS * TOPK_TOKENS) == 0 else 1
    n_part = n // parts
    idx, gate = peer_topk(qry, keys, n_part)
    for i in range(parts):
        side = (qry, keys, (i + 1) * n_part) if i + 1 < parts else None
        res = peer_experts(x2, idx, gate, p['norm_ffn'][l], p['norm_final'], w['peer_uv'][l], final_norm,
                           row0=i * n_part, side=side)
        x2, idx, gate = res if side is not None else (res, None, None)
    return x2


def _trunk(x, st_a_conv, st_a_delta, st_b_conv, st_b_h, c_k, c_v, p, w, past):
    b, t, d = x.shape
    n = b * t
    chunk = min(CHUNK, t)
    depth = p['norm_mix'].shape[0]
    x2 = x.reshape(n, d)
    a_conv, a_delta, b_conv, b_h, new_k, new_v = [], [], [], [], [], []
    for l in range(depth):
        j = l // 2
        if l % 2 == 0:
            proj = norm_matmul(x2, p['norm_mix'][l], w['ab_in'][j]).reshape(b, t, -1)
            o_a, s_new = gdn(proj, st_a_conv[j], st_a_delta[j], p['a_conv_w'][j], p['a_log'][j],
                             p['a_dt_bias'][j], p['a_norm'][j], chunk)
            o_b, h_last = lru(proj, st_b_conv[j], st_b_h[j], p['b_conv_w'][j], p['b_conv_b'][j], w['ga'][j],
                              p['b_gate_a_b'][j], w['gx'][j], p['b_gate_x_b'][j], p['b_lambda'][j])
            a3 = 3 * A_WIDTH
            xb0 = a3 + A_WIDTH
            a_conv.append(jnp.concatenate([st_a_conv[j], proj[:, :, :a3]], axis=1)[:, -(CONV_W - 1):])
            b_conv.append(jnp.concatenate([st_b_conv[j], proj[:, :, xb0:xb0 + B_WIDTH]], axis=1)[:, -(CONV_W - 1):])
            a_delta.append(s_new)
            b_h.append(h_last)
            x2 = matmul_res([o_a.reshape(n, A_WIDTH), o_b.reshape(n, B_WIDTH)],
                            [w['ab_out_a'][j], w['ab_out_b'][j]], x2)
        else:
            lam_init = 0.8 - 0.6 * math.exp(-0.3 * l)
            qkv = norm_matmul(x2, p['norm_mix'][l], w['c_qkv'][j]).reshape(b, t, 3 * C_WIDTH)
            kn = qkv[:, :, C_WIDTH:2 * C_WIDTH]
            vn = qkv[:, :, 2 * C_WIDTH:]
            kp = None if c_k is None else c_k[j].reshape(b, -1, C_WIDTH)
            vp = None if c_v is None else c_v[j].reshape(b, -1, C_WIDTH)
            assert past == (0 if kp is None else kp.shape[1])
            o_c = diff_attn(qkv, kp, vp, w['c_lam'][j], p['c_norm'][j], lam_init)
            new_k.append(kn.reshape(b, t, C_HEADS, 2, C_DH))
            new_v.append(vn.reshape(b, t, C_HEADS, C_VD))
            x2 = matmul_res([o_c.reshape(n, C_WIDTH)], [w['c_out'][j]], x2)
        x2 = _peer(x2, l, p, w, final_norm=(l == depth - 1))
    return (x2.reshape(b, t, d), jnp.stack(a_conv), jnp.stack(a_delta), jnp.stack(b_conv), jnp.stack(b_h),
            jnp.stack(new_k), jnp.stack(new_v))


def kernel(x_prompt, x_sample, state_a_conv, state_a_delta, state_b_conv, state_b_rglru, cache_c_k, cache_c_v, norm_mix, norm_ffn, norm_final, ab_w_in, ab_w_out, a_conv_w, a_log, a_dt_bias, a_norm, b_conv_w, b_conv_b, b_gate_a_w, b_gate_a_b, b_gate_x_w, b_gate_x_b, b_lambda, c_w_qkv, c_w_out, c_lambda_q1, c_lambda_k1, c_lambda_q2, c_lambda_k2, c_norm, peer_w_query, peer_sub_keys, peer_u, peer_v):
    p = dict(norm_mix=norm_mix, norm_ffn=norm_ffn, norm_final=norm_final, ab_w_in=ab_w_in,
             ab_w_out=ab_w_out, a_conv_w=a_conv_w, a_log=a_log, a_dt_bias=a_dt_bias, a_norm=a_norm,
             b_conv_w=b_conv_w, b_conv_b=b_conv_b, b_gate_a_w=b_gate_a_w, b_gate_a_b=b_gate_a_b,
             b_gate_x_w=b_gate_x_w, b_gate_x_b=b_gate_x_b, b_lambda=b_lambda, c_w_qkv=c_w_qkv,
             c_w_out=c_w_out, c_lambda_q1=c_lambda_q1, c_lambda_k1=c_lambda_k1,
             c_lambda_q2=c_lambda_q2, c_lambda_k2=c_lambda_k2, c_norm=c_norm,
             peer_w_query=peer_w_query, peer_sub_keys=peer_sub_keys, peer_u=peer_u, peer_v=peer_v)
    w = _prep_weights(p)
    bp = x_prompt.shape[0]
    dt = x_prompt.dtype
    n_even = state_a_conv.shape[0]
    z_a_conv = jnp.zeros((n_even, bp) + state_a_conv.shape[2:], dt)
    z_a_delta = jnp.zeros((n_even, bp) + state_a_delta.shape[2:], dt)
    z_b_conv = jnp.zeros((n_even, bp) + state_b_conv.shape[2:], dt)
    z_b_h = jnp.zeros((n_even, bp) + state_b_rglru.shape[2:], dt)
    (y_prompt, pa_conv, pa_delta, pb_conv, pb_h, pc_k, pc_v) = _trunk(
        x_prompt, z_a_conv, z_a_delta, z_b_conv, z_b_h, None, None, p, w, 0)
    (y_sample, sa_conv, sa_delta, sb_conv, sb_h, sc_k, sc_v) = _trunk(
        x_sample, state_a_conv, state_a_delta, state_b_conv, state_b_rglru, cache_c_k, cache_c_v,
        p, w, cache_c_k.shape[2])
    return (y_prompt, y_sample, pa_conv, pa_delta, pb_conv, pb_h, pc_k, pc_v,
            sa_conv, sa_delta, sb_conv, sb_h, sc_k, sc_v)
```
